```python
import jax, jax.numpy as jnp
from jax import lax
import numpy as np

D_MODEL = 2048
BATCH = 4
SEQ = 4096
DEPTH = 4

CHUNK = 64
M_HEADS = 4
M_HEAD_DIM = 256
D_M = M_HEADS * M_HEAD_DIM
CONV_K = 4
POOL_WINDOWS = (2, 4, 8, 16)
POOL_GROUP = 256
D_P = len(POOL_WINDOWS) * POOL_GROUP
N_BRANCH = 2
D_FF = 5632
N_EXPERTS = 8
TOP_K = 2
EXPERT_FF = 5632
N_DENSE = (DEPTH + 1) // 2
N_MOE = DEPTH // 2
EPS = 1e-6
COL_Q = 0
COL_K = D_M
COL_V = 2 * D_M
COL_O = 3 * D_M
COL_I = 4 * D_M
COL_F = COL_I + M_HEADS
COL_P = COL_F + M_HEADS
COL_G = COL_P + D_P
IN_COLS = COL_G + N_BRANCH * D_MODEL

kernel_name = "hybrid_mlstm_pool_moe_trunk"


def rmsnorm(x, g):
    xf = x.astype(jnp.float32)
    xf = xf * lax.rsqrt(jnp.mean(xf * xf, axis=-1, keepdims=True) + EPS)
    return (xf * g.astype(jnp.float32)).astype(x.dtype)


def head_rmsnorm(h, g):
    B, S, _ = h.shape
    hf = h.reshape(B, S, M_HEADS, M_HEAD_DIM)
    hf = hf * lax.rsqrt(jnp.mean(hf * hf, axis=-1, keepdims=True) + EPS)
    return hf.reshape(B, S, D_M) * g.astype(jnp.float32)


def causal_conv_silu(u, w):
    S = u.shape[1]
    up = jnp.pad(u, ((0, 0), (CONV_K - 1, 0), (0, 0)))
    out = up[:, 0:S] * w[0]
    for j in range(1, CONV_K):
        out = out + up[:, j:j + S] * w[j]
    return jax.nn.silu(out)


def mlstm_chunkwise(q, k, v, ig, fg):
    B, S, _ = q.shape
    nc = S // CHUNK
    f32 = jnp.float32

    def to_chunks(a):
        return a.astype(f32).reshape(B, nc, CHUNK, M_HEADS, M_HEAD_DIM).transpose(1, 0, 3, 2, 4)

    def gate_chunks(a):
        return a.reshape(B, nc, CHUNK, M_HEADS).transpose(1, 0, 3, 2)

    qc = to_chunks(q)
    kc = to_chunks(k) * (M_HEAD_DIM ** -0.5)
    vc = to_chunks(v)
    li = gate_chunks(ig.astype(f32))
    lf = gate_chunks(jax.nn.log_sigmoid(fg.astype(f32)))
    causal = jnp.tril(jnp.ones((CHUNK, CHUNK), dtype=bool))

    def step(carry, inp):
        C, n, m = carry
        qb, kb, vb, lib, lfb = inp
        b = jnp.cumsum(lfb, axis=-1)
        Dm = b[..., :, None] - b[..., None, :] + lib[..., None, :]
        Dm = jnp.where(causal, Dm, -jnp.inf)
        inter = b + m[..., None]
        m_t = jnp.maximum(inter, jnp.max(Dm, axis=-1))
        w_inter = jnp.exp(inter - m_t)
        s = jnp.einsum('bhtd,bhsd->bhts', qb, kb) * jnp.exp(Dm - m_t[..., None])
        num = w_inter[..., None] * jnp.einsum('bhvk,bhtk->bhtv', C, qb) + jnp.einsum('bhts,bhsv->bhtv', s, vb)
        den = w_inter * jnp.einsum('bhk,bhtk->bht', n, qb) + jnp.sum(s, axis=-1)
        h = num / jnp.maximum(jnp.abs(den), jnp.exp(-m_t))[..., None]
        bL = b[..., -1]
        g = bL[..., None] - b + lib
        m_new = jnp.maximum(bL + m, jnp.max(g, axis=-1))
        a = jnp.exp(bL + m - m_new)
        ws = jnp.exp(g - m_new[..., None])
        C_new = a[..., None, None] * C + jnp.einsum('bhs,bhsv,bhsk->bhvk', ws, vb, kb)
        n_new = a[..., None] * n + jnp.einsum('bhs,bhsk->bhk', ws, kb)
        return (C_new, n_new, m_new), h

    init = (jnp.zeros((B, M_HEADS, M_HEAD_DIM, M_HEAD_DIM), f32),
            jnp.zeros((B, M_HEADS, M_HEAD_DIM), f32),
            jnp.zeros((B, M_HEADS), f32))
    _, hc = lax.scan(step, init, (qc, kc, vc, li, lf))
    return hc.transpose(1, 0, 3, 2, 4).reshape(B, S, D_M)


def pool_mixer(u, pool_w, pool_scale):
    S = u.shape[1]
    uf = u.astype(jnp.float32)
    pos = jnp.arange(S, dtype=jnp.float32)
    outs = []
    for j, w in enumerate(POOL_WINDOWS):
        p = uf[..., j * POOL_GROUP:(j + 1) * POOL_GROUP]
        cs = jnp.cumsum(p, axis=1)
        cs_pad = jnp.pad(cs, ((0, 0), (w, 0), (0, 0)))
        win = cs_pad[:, w:] - cs_pad[:, :-w]
        count = jnp.minimum(pos + 1.0, float(w))
        pooled = win / count[None, :, None] - p
        outs.append(jnp.einsum('bsc,cd->bsd', pooled, pool_w[j].astype(jnp.float32)))
    return jnp.concatenate(outs, axis=-1) * pool_scale.astype(jnp.float32)


def swiglu(t, wg, wu, wd):
    return (jax.nn.silu(t @ wg) * (t @ wu)) @ wd


def moe_swiglu(h, router_w, router_b, wg, wu, wd):
    B, S, D = h.shape
    t = h.reshape(B * S, D)
    logits = (t @ router_w).astype(jnp.float32) + router_b.astype(jnp.float32)
    top_v, top_i = lax.top_k(logits, TOP_K)
    gates = jax.nn.softmax(top_v, axis=-1)
    combine = jnp.sum(jax.nn.one_hot(top_i, N_EXPERTS, dtype=jnp.float32) * gates[..., None], axis=1)
    combine = combine.astype(t.dtype)
    out = jnp.zeros_like(t)
    for e in range(N_EXPERTS):
        out = out + combine[:, e:e + 1] * swiglu(t, wg[e], wu[e], wd[e])
    return out.reshape(B, S, D)


def hybrid_mixer(xn, w_in, gate_b, conv_w, m_norm_g, pool_w, pool_scale, w_bm, w_bp, w_out):
    dt = xn.dtype
    z = xn @ w_in
    q = causal_conv_silu(z[..., COL_Q:COL_K], conv_w[:, :D_M])
    k = causal_conv_silu(z[..., COL_K:COL_V], conv_w[:, D_M:])
    v = z[..., COL_V:COL_O]
    o = jax.nn.sigmoid(z[..., COL_O:COL_I].astype(jnp.float32))
    ig = z[..., COL_I:COL_F] + gate_b[:M_HEADS]
    fg = z[..., COL_F:COL_P] + gate_b[M_HEADS:]
    h_m = o * head_rmsnorm(mlstm_chunkwise(q, k, v, ig, fg), m_norm_g)
    h_p = pool_mixer(z[..., COL_P:COL_G], pool_w, pool_scale)
    g = jax.nn.sigmoid(z[..., COL_G:])
    g_m, g_p = g[..., :D_MODEL], g[..., D_MODEL:]
    y = g_m * (h_m.astype(dt) @ w_bm) + g_p * (h_p.astype(dt) @ w_bp)
    return y @ w_out


def setup_inputs(seed: int = 0) -> dict:
    key = jax.random.key(seed)
    ks = jax.random.split(key, 24)
    f32 = jnp.float32

    def nrm(k, shape, fan):
        return jax.random.normal(k, shape, f32) * (fan ** -0.5)

    def gain(k, shape):
        return 1.0 + 0.02 * jax.random.normal(k, shape, f32)

    i_bias = 0.1 * jax.random.normal(ks[5], (DEPTH, M_HEADS), f32)
    f_bias = jnp.linspace(3.0, 6.0, M_HEADS, dtype=f32)[None, :] + 0.1 * jax.random.normal(ks[6], (DEPTH, M_HEADS), f32)
    return {
        "x": jax.random.normal(ks[0], (BATCH, SEQ, D_MODEL), f32),
        "norm1_g": gain(ks[1], (DEPTH, D_MODEL)),
        "norm2_g": gain(ks[2], (DEPTH, D_MODEL)),
        "final_g": gain(ks[3], (D_MODEL,)),
        "w_in": nrm(ks[4], (DEPTH, D_MODEL, IN_COLS), D_MODEL),
        "mlstm_gate_b": jnp.concatenate([i_bias, f_bias], axis=-1),
        "conv_w": nrm(ks[7], (DEPTH, CONV_K, 2 * D_M), CONV_K),
        "mlstm_norm_g": gain(ks[8], (DEPTH, D_M)),
        "pool_w": nrm(ks[9], (DEPTH, len(POOL_WINDOWS), POOL_GROUP, POOL_GROUP), POOL_GROUP),
        "pool_scale": gain(ks[10], (DEPTH, D_P)),
        "w_branch_m": nrm(ks[11], (DEPTH, D_M, D_MODEL), D_M),
        "w_branch_p": nrm(ks[12], (DEPTH, D_P, D_MODEL), D_P),
        "w_out": nrm(ks[13], (DEPTH, D_MODEL, D_MODEL), D_MODEL),
        "ffn_w_gate": nrm(ks[14], (N_DENSE, D_MODEL, D_FF), D_MODEL),
        "ffn_w_up": nrm(ks[15], (N_DENSE, D_MODEL, D_FF), D_MODEL),
        "ffn_w_down": nrm(ks[16], (N_DENSE, D_FF, D_MODEL), D_FF),
        "router_w": nrm(ks[17], (N_MOE, D_MODEL, N_EXPERTS), D_MODEL),
        "router_b": 0.01 * jax.random.normal(ks[18], (N_MOE, N_EXPERTS), f32),
        "exp_w_gate": nrm(ks[19], (N_MOE, N_EXPERTS, D_MODEL, EXPERT_FF), D_MODEL),
        "exp_w_up": nrm(ks[20], (N_MOE, N_EXPERTS, D_MODEL, EXPERT_FF), D_MODEL),
        "exp_w_down": nrm(ks[21], (N_MOE, N_EXPERTS, EXPERT_FF, D_MODEL), EXPERT_FF),
    }


def reference(x, norm1_g, norm2_g, final_g, w_in, mlstm_gate_b, conv_w, mlstm_norm_g,
              pool_w, pool_scale, w_branch_m, w_branch_p, w_out,
              ffn_w_gate, ffn_w_up, ffn_w_down,
              router_w, router_b, exp_w_gate, exp_w_up, exp_w_down):
    h = x
    for l in range(DEPTH):
        xn = rmsnorm(h, norm1_g[l])
        h = h + hybrid_mixer(xn, w_in[l], mlstm_gate_b[l], conv_w[l], mlstm_norm_g[l],
                             pool_w[l], pool_scale[l], w_branch_m[l], w_branch_p[l], w_out[l])
        xn = rmsnorm(h, norm2_g[l])
        if l % 2 == 0:
            j = l // 2
            h = h + swiglu(xn, ffn_w_gate[j], ffn_w_up[j], ffn_w_down[j])
        else:
            j = l // 2
            h = h + moe_swiglu(xn, router_w[j], router_b[j], exp_w_gate[j], exp_w_up[j], exp_w_down[j])
    return rmsnorm(h, final_g)
```

```python
import functools

import jax
import jax.numpy as jnp
from jax import lax
from jax.experimental import pallas as pl
from jax.experimental.pallas import tpu as pltpu

F32 = jnp.float32
BF16 = jnp.bfloat16

EPS = 1e-6
CONV_K = 4
POOL_WINDOWS = (2, 4, 8, 16)
POOL_HALO = 16
TOP_K = 2
LANES = 128
SUBLANES = 8
NEG_BIG = -1e30
VMEM_LIMIT = 56 * 1024 * 1024

INPROJ_TM = 1024
INPROJ_TN = 1024
MLSTM_T = 512
MLSTM_CHUNK = 64
POOL_T = 512
MERGE_TM = 512
FFN_TM = 1024
FFN_TF = 256
MOE_TM = 1024
COMBINE_TM = 256
NORM_TM = 1024


def _cparams(n_axes):
    return pltpu.CompilerParams(
        dimension_semantics=("arbitrary",) * n_axes,
        vmem_limit_bytes=VMEM_LIMIT,
    )


def _rms(x, g):
    ms = jnp.mean(x * x, axis=-1, keepdims=True)
    return x * lax.rsqrt(ms + EPS) * g


def _sigmoid(x):
    return 1.0 / (1.0 + jnp.exp(-x))


def _inproj_kernel(x_ref, g_ref, w_ref, wif_ref, z_ref, gates_ref, xn_ref):
    @pl.when(pl.program_id(1) == 0)
    def _():
        xn = _rms(x_ref[...], g_ref[...]).astype(BF16)
        xn_ref[...] = xn
        gates_ref[...] = jnp.dot(xn, wif_ref[...], preferred_element_type=F32)

    z_ref[...] = jnp.dot(xn_ref[...], w_ref[...],
                         preferred_element_type=F32).astype(BF16)


def _inproj(h, g, w_main, w_if):
    n, d = h.shape
    cols = w_main.shape[1]
    tm = min(INPROJ_TM, n)
    tn = min(INPROJ_TN, cols)
    return pl.pallas_call(
        _inproj_kernel,
        grid=(n // tm, cols // tn),
        in_specs=[
            pl.BlockSpec((tm, d), lambda i, j: (i, 0)),
            pl.BlockSpec((1, d), lambda i, j: (0, 0)),
            pl.BlockSpec((d, tn), lambda i, j: (0, j)),
            pl.BlockSpec((d, LANES), lambda i, j: (0, 0)),
        ],
        out_specs=[
            pl.BlockSpec((tm, tn), lambda i, j: (i, j)),
            pl.BlockSpec((tm, LANES), lambda i, j: (i, 0)),
        ],
        out_shape=[
            jax.ShapeDtypeStruct((n, cols), BF16),
            jax.ShapeDtypeStruct((n, LANES), F32),
        ],
        scratch_shapes=[pltpu.VMEM((tm, d), BF16)],
        compiler_params=_cparams(2),
        name="inproj",
    )(h, g.reshape(1, d), w_main, w_if)


def _mlstm_kernel(q_ref, k_ref, v_ref, o_ref, gt_ref, gb_ref, cwq_ref, cwk_ref,
                  ng_ref, out_ref, ct_ref, n_ref, m_ref, qh_ref, kh_ref,
                  *, chunk, nheads):
    hd = pl.program_id(1)
    t_rows, dh = q_ref.shape

    @pl.when(pl.program_id(2) == 0)
    def _():
        ct_ref[...] = jnp.zeros_like(ct_ref)
        n_ref[...] = jnp.zeros_like(n_ref)
        m_ref[...] = jnp.zeros_like(m_ref)
        qh_ref[...] = jnp.zeros_like(qh_ref)
        kh_ref[...] = jnp.zeros_like(kh_ref)

    def conv_silu(u_ref, halo_ref, cw_ref):
        u = u_ref[...].astype(F32)
        ext = jnp.concatenate([halo_ref[...], u], axis=0)
        w = cw_ref[...]
        acc = w[CONV_K - 1:CONV_K] * u
        for d in range(1, CONV_K):
            shifted = pltpu.roll(ext, d, axis=0)[SUBLANES:]
            acc = acc + w[CONV_K - 1 - d:CONV_K - d] * shifted
        halo_ref[...] = u[t_rows - SUBLANES:]
        return acc * _sigmoid(acc)

    q = conv_silu(q_ref, qh_ref, cwq_ref)
    k = conv_silu(k_ref, kh_ref, cwk_ref) * (dh ** -0.5)

    gts = gt_ref[...] + gb_ref[...]
    lf = jnp.minimum(gts, 0.0) - jnp.log(1.0 + jnp.exp(-jnp.abs(gts)))
    row = lax.broadcasted_iota(jnp.int32, gts.shape, 0)
    lane = lax.broadcasted_iota(jnp.int32, gts.shape, 1)
    in_chunk = row & (chunk - 1)
    b = lf
    d = 1
    while d < chunk:
        b = b + jnp.where(in_chunk >= d, pltpu.roll(b, d, axis=0), 0.0)
        d *= 2
    icol = jnp.sum(jnp.where(lane == hd, gts, 0.0), axis=1, keepdims=True)
    bcol = jnp.sum(jnp.where(lane == hd + nheads, b, 0.0), axis=1, keepdims=True)
    sub = lax.broadcasted_iota(jnp.int32, (gts.shape[1], t_rows), 0)
    irow = jnp.sum(jnp.where(sub == hd, gts.T, 0.0), axis=0, keepdims=True)
    brow = jnp.sum(jnp.where(sub == hd + nheads, b.T, 0.0), axis=0, keepdims=True)

    tt = lax.broadcasted_iota(jnp.int32, (chunk, chunk), 0)
    ss = lax.broadcasted_iota(jnp.int32, (chunk, chunk), 1)
    causal = tt >= ss

    ct = ct_ref[...]
    nvec = n_ref[...]
    m_prev = m_ref[...]
    ng = ng_ref[...]
    for c in range(t_rows // chunk):
        lo, hi = c * chunk, (c + 1) * chunk
        qc, kc = q[lo:hi], k[lo:hi]
        vc = v_ref[lo:hi, :]
        bc, ic = bcol[lo:hi], icol[lo:hi]
        br, ir = brow[:, lo:hi], irow[:, lo:hi]
        dm = jnp.where(causal, bc - br + ir, NEG_BIG)
        inter = bc + m_prev
        mt = jnp.maximum(inter, jnp.max(dm, axis=1, keepdims=True))
        wi = jnp.exp(inter - mt)
        qb, kb = qc.astype(BF16), kc.astype(BF16)
        s = lax.dot_general(qb, kb, (((1,), (1,)), ((), ())),
                            preferred_element_type=F32) * jnp.exp(dm - mt)
        num = wi * jnp.dot(qb, ct.astype(BF16), preferred_element_type=F32)
        num = num + jnp.dot(s.astype(BF16), vc, preferred_element_type=F32)
        den = wi * jnp.sum(qc * nvec, axis=1, keepdims=True)
        den = den + jnp.sum(s, axis=1, keepdims=True)
        hr = num / jnp.maximum(jnp.abs(den), jnp.exp(-mt))
        hn = _rms(hr, ng)
        og = _sigmoid(o_ref[lo:hi, :].astype(F32))
        out_ref[lo:hi, :] = (og * hn).astype(BF16)
        bl = bc[chunk - 1:chunk]
        gcol = bl - bc + ic
        m_new = jnp.maximum(bl + m_prev, jnp.max(gcol, axis=0, keepdims=True))
        a = jnp.exp(bl + m_prev - m_new)
        kw = kc * jnp.exp(gcol - m_new)
        ct = a * ct + lax.dot_general(kw.astype(BF16), vc, (((0,), (0,)), ((), ())),
                                      preferred_element_type=F32)
        nvec = a * nvec + jnp.sum(kw, axis=0, keepdims=True)
        m_prev = m_new
    ct_ref[...] = ct
    n_ref[...] = nvec
    m_ref[...] = m_prev


def _mlstm(z, gates, gate_b, conv_w, norm_g, *, batch, seq, nheads, dh):
    n = z.shape[0]
    t = min(MLSTM_T, seq)
    chunk = min(MLSTM_CHUNK, t)
    spb = seq // t
    dm = nheads * dh
    gb = jnp.zeros((1, LANES), F32).at[0, :2 * nheads].set(gate_b)
    cw = jnp.zeros((SUBLANES, 2 * dm), F32).at[:CONV_K].set(conv_w)

    def zspec(col0):
        return pl.BlockSpec((t, dh), lambda b, h, s: (b * spb + s, col0 + h))

    return pl.pallas_call(
        functools.partial(_mlstm_kernel, chunk=chunk, nheads=nheads),
        grid=(batch, nheads, spb),
        in_specs=[
            zspec(0), zspec(nheads), zspec(2 * nheads), zspec(3 * nheads),
            pl.BlockSpec((t, LANES), lambda b, h, s: (b * spb + s, 0)),
            pl.BlockSpec((1, LANES), lambda b, h, s: (0, 0)),
            pl.BlockSpec((SUBLANES, dh), lambda b, h, s: (0, h)),
            pl.BlockSpec((SUBLANES, dh), lambda b, h, s: (0, nheads + h)),
            pl.BlockSpec((1, dh), lambda b, h, s: (0, h)),
        ],
        out_specs=pl.BlockSpec((t, dh), lambda b, h, s: (b * spb + s, h)),
        out_shape=jax.ShapeDtypeStruct((n, dm), BF16),
        scratch_shapes=[
            pltpu.VMEM((dh, dh), F32),
            pltpu.VMEM((1, dh), F32),
            pltpu.VMEM((1, 1), F32),
            pltpu.VMEM((SUBLANES, dh), F32),
            pltpu.VMEM((SUBLANES, dh), F32),
        ],
        compiler_params=_cparams(3),
        name="mlstm",
    )(z, z, z, z, gates, gb, cw, cw, norm_g.reshape(1, dm))


def _pool_kernel(p_ref, halo_ref, w_ref, sc_ref, out_ref, *, tiles_per_seq):
    t_rows = p_ref.shape[0]
    grp = w_ref.shape[1]
    pos0 = (pl.program_id(0) % tiles_per_seq) * t_rows
    p = p_ref[...].astype(F32)
    halo = jnp.where(pos0 == 0, 0.0, halo_ref[...].astype(F32))
    ext = jnp.concatenate([halo, p], axis=0)
    pos = pos0 + lax.broadcasted_iota(jnp.int32, (t_rows, 1), 0)
    posf = (pos + 1).astype(F32)
    for j, w in enumerate(POOL_WINDOWS):
        cs = slice(j * grp, (j + 1) * grp)
        win = ext[:, cs]
        d = 1
        while d < w:
            win = win + pltpu.roll(win, d, axis=0)
            d *= 2
        pooled = win[POOL_HALO:] / jnp.minimum(posf, float(w)) - p[:, cs]
        y = jnp.dot(pooled.astype(BF16), w_ref[j], preferred_element_type=F32)
        out_ref[:, cs] = (y * sc_ref[:, cs]).astype(BF16)


def _pool(z, pool_w, pool_scale, *, seq, col_block):
    n = z.shape[0]
    ngrp, grp, _ = pool_w.shape
    dp = ngrp * grp
    t = min(POOL_T, seq)
    tps = seq // t
    hb = t // POOL_HALO
    return pl.pallas_call(
        functools.partial(_pool_kernel, tiles_per_seq=tps),
        grid=(n // t,),
        in_specs=[
            pl.BlockSpec((t, dp), lambda i: (i, col_block)),
            pl.BlockSpec((POOL_HALO, dp), lambda i: (jnp.maximum(i * hb - 1, 0), col_block)),
            pl.BlockSpec((ngrp, grp, grp), lambda i: (0, 0, 0)),
            pl.BlockSpec((1, dp), lambda i: (0, 0)),
        ],
        out_specs=pl.BlockSpec((t, dp), lambda i: (i, 0)),
        out_shape=jax.ShapeDtypeStruct((n, dp), BF16),
        compiler_params=_cparams(1),
        name="pool",
    )(z, z, pool_w, pool_scale.reshape(1, dp))


def _merge_kernel(h_ref, hm_ref, hp_ref, gm_ref, gp_ref, wbm_ref, wbp_ref, wo_ref, out_ref):
    ym = jnp.dot(hm_ref[...], wbm_ref[...], preferred_element_type=F32)
    yp = jnp.dot(hp_ref[...], wbp_ref[...], preferred_element_type=F32)
    y = _sigmoid(gm_ref[...].astype(F32)) * ym + _sigmoid(gp_ref[...].astype(F32)) * yp
    out_ref[...] = h_ref[...] + jnp.dot(y.astype(BF16), wo_ref[...],
                                        preferred_element_type=F32)


def _merge(h, hm, hp, z, wbm, wbp, wo, *, gate_block):
    n, d = h.shape
    dm, dp = hm.shape[1], hp.shape[1]
    tm = min(MERGE_TM, n)
    once = pl.Buffered(1)
    return pl.pallas_call(
        _merge_kernel,
        grid=(n // tm,),
        in_specs=[
            pl.BlockSpec((tm, d), lambda i: (i, 0)),
            pl.BlockSpec((tm, dm), lambda i: (i, 0)),
            pl.BlockSpec((tm, dp), lambda i: (i, 0)),
            pl.BlockSpec((tm, d), lambda i: (i, gate_block)),
            pl.BlockSpec((tm, d), lambda i: (i, gate_block + 1)),
            pl.BlockSpec((dm, d), lambda i: (0, 0), pipeline_mode=once),
            pl.BlockSpec((dp, d), lambda i: (0, 0), pipeline_mode=once),
            pl.BlockSpec((d, d), lambda i: (0, 0), pipeline_mode=once),
        ],
        out_specs=pl.BlockSpec((tm, d), lambda i: (i, 0)),
        out_shape=jax.ShapeDtypeStruct((n, d), F32),
        compiler_params=_cparams(1),
        name="merge",
    )(h, hm, hp, z, z, wbm, wbp, wo)


def _swiglu_step(xn_ref, wg_ref, wu_ref, wd_ref, out_ref):
    xn = xn_ref[...]
    a = jnp.dot(xn, wg_ref[...].astype(BF16), preferred_element_type=F32)
    u = jnp.dot(xn, wu_ref[...].astype(BF16), preferred_element_type=F32)
    act = (a * _sigmoid(a) * u).astype(BF16)
    out_ref[...] += jnp.dot(act, wd_ref[...].astype(BF16), preferred_element_type=F32)


def _ffn_kernel(x_ref, g_ref, wg_ref, wu_ref, wd_ref, out_ref, xn_ref):
    @pl.when(pl.program_id(1) == 0)
    def _():
        x = x_ref[...]
        xn_ref[...] = _rms(x, g_ref[...]).astype(BF16)
        out_ref[...] = x

    _swiglu_step(xn_ref, wg_ref, wu_ref, wd_ref, out_ref)


def _ffn(h, g, wg, wu, wd):
    n, d = h.shape
    ff = wg.shape[1]
    tm = min(FFN_TM, n)
    tf = min(FFN_TF, ff)
    return pl.pallas_call(
        _ffn_kernel,
        grid=(n // tm, ff // tf),
        in_specs=[
            pl.BlockSpec((tm, d), lambda i, f: (i, 0), pipeline_mode=pl.Buffered(1)),
            pl.BlockSpec((1, d), lambda i, f: (0, 0)),
            pl.BlockSpec((d, tf), lambda i, f: (0, f)),
            pl.BlockSpec((d, tf), lambda i, f: (0, f)),
            pl.BlockSpec((tf, d), lambda i, f: (f, 0)),
        ],
        out_specs=pl.BlockSpec((tm, d), lambda i, f: (i, 0)),
        out_shape=jax.ShapeDtypeStruct((n, d), F32),
        scratch_shapes=[pltpu.VMEM((tm, d), BF16)],
        compiler_params=_cparams(2),
        name="ffn",
    )(h, g.reshape(1, d), wg, wu, wd)


def _gffn_kernel(te_ref, nv_ref, x_ref, g_ref, wg_ref, wu_ref, wd_ref, out_ref, xn_ref):
    @pl.when(pl.program_id(0) < nv_ref[0])
    def _():
        @pl.when(pl.program_id(1) == 0)
        def _():
            xn_ref[...] = _rms(x_ref[...], g_ref[...]).astype(BF16)
            out_ref[...] = jnp.zeros_like(out_ref)

        _swiglu_step(xn_ref, wg_ref, wu_ref, wd_ref, out_ref)

    @pl.when(jnp.logical_and(pl.program_id(0) >= nv_ref[0], pl.program_id(1) == 0))
    def _():
        out_ref[...] = jnp.zeros_like(out_ref)


def _grouped_ffn(xs, g, wg, wu, wd, tile_expert, n_valid, *, tm):
    slots, d = xs.shape
    ff = wg.shape[2]
    tf = min(FFN_TF, ff)
    nf = ff // tf
    ntiles = slots // tm

    def row(t, nv):
        return jnp.minimum(t, nv[0] - 1)

    def fcol(t, f, nv):
        return jnp.where(t < nv[0], f, nf - 1)

    grid_spec = pltpu.PrefetchScalarGridSpec(
        num_scalar_prefetch=2,
        grid=(ntiles, nf),
        in_specs=[
            pl.BlockSpec((tm, d), lambda t, f, te, nv: (row(t, nv), 0),
                         pipeline_mode=pl.Buffered(1)),
            pl.BlockSpec((1, d), lambda t, f, te, nv: (0, 0)),
            pl.BlockSpec((None, d, tf), lambda t, f, te, nv: (te[t], 0, fcol(t, f, nv))),
            pl.BlockSpec((None, d, tf), lambda t, f, te, nv: (te[t], 0, fcol(t, f, nv))),
            pl.BlockSpec((None, tf, d), lambda t, f, te, nv: (te[t], fcol(t, f, nv), 0)),
        ],
        out_specs=pl.BlockSpec((tm, d), lambda t, f, te, nv: (t, 0)),
        scratch_shapes=[pltpu.VMEM((tm, d), BF16)],
    )
    return pl.pallas_call(
        _gffn_kernel,
        grid_spec=grid_spec,
        out_shape=jax.ShapeDtypeStruct((slots, d), F32),
        compiler_params=_cparams(2),
        name="grouped_ffn",
    )(tile_expert, n_valid, xs, g.reshape(1, d), wg, wu, wd)


def _router_kernel(x_ref, g_ref, wh_ref, wl_ref, rb_ref, out_ref, *, nexp):
    xn = _rms(x_ref[...], g_ref[...])
    xh = xn.astype(BF16)
    xl = (xn - xh.astype(F32)).astype(BF16)
    wh, wl = wh_ref[...], wl_ref[...]
    logits = (jnp.dot(xh, wh, preferred_element_type=F32)
              + jnp.dot(xh, wl, preferred_element_type=F32)
              + jnp.dot(xl, wh, preferred_element_type=F32)) + rb_ref[...]
    lane = lax.broadcasted_iota(jnp.int32, logits.shape, 1)
    logits = jnp.where(lane < nexp, logits, NEG_BIG)
    v1 = jnp.max(logits, axis=1, keepdims=True)
    i1 = jnp.min(jnp.where(logits == v1, lane, LANES), axis=1, keepdims=True)
    rest = jnp.where(lane == i1, NEG_BIG, logits)
    v2 = jnp.max(rest, axis=1, keepdims=True)
    i2 = jnp.min(jnp.where(rest == v2, lane, LANES), axis=1, keepdims=True)
    e2 = jnp.exp(v2 - v1)
    g1 = 1.0 / (1.0 + e2)
    g2 = e2 / (1.0 + e2)
    out = jnp.where(lane == 0, i1.astype(F32), 0.0)
    out = jnp.where(lane == 1, i2.astype(F32), out)
    out = jnp.where(lane == 2, g1, out)
    out = jnp.where(lane == 3, g2, out)
    out_ref[...] = out


def _router(h, g, router_w, router_b):
    n, d = h.shape
    nexp = router_w.shape[1]
    tm = min(NORM_TM, n)
    w = jnp.zeros((d, LANES), F32).at[:, :nexp].set(router_w)
    wh = w.astype(BF16)
    wl = (w - wh.astype(F32)).astype(BF16)
    rb = jnp.zeros((1, LANES), F32).at[0, :nexp].set(router_b)
    return pl.pallas_call(
        functools.partial(_router_kernel, nexp=nexp),
        grid=(n // tm,),
        in_specs=[
            pl.BlockSpec((tm, d), lambda i: (i, 0)),
            pl.BlockSpec((1, d), lambda i: (0, 0)),
            pl.BlockSpec((d, LANES), lambda i: (0, 0)),
            pl.BlockSpec((d, LANES), lambda i: (0, 0)),
            pl.BlockSpec((1, LANES), lambda i: (0, 0)),
        ],
        out_specs=pl.BlockSpec((tm, LANES), lambda i: (i, 0)),
        out_shape=jax.ShapeDtypeStruct((n, LANES), F32),
        compiler_params=_cparams(1),
        name="router",
    )(h, g.reshape(1, d), wh, wl, rb)


def _gather_kernel(src_ref, nv_ref, x_hbm, out_ref, sem, *, tm):
    t = pl.program_id(0)

    @pl.when(t < nv_ref[0])
    def _():
        base = t * tm

        def issue(r, carry):
            tok = src_ref[base + r]
            pltpu.make_async_copy(x_hbm.at[pl.ds(tok, 1)], out_ref.at[pl.ds(r, 1)], sem).start()
            return carry

        lax.fori_loop(0, tm, issue, 0)
        pltpu.make_async_copy(x_hbm.at[pl.ds(0, tm)], out_ref, sem).wait()

    @pl.when(t >= nv_ref[0])
    def _():
        out_ref[...] = jnp.zeros_like(out_ref)


def _gather_rows(h, src_token, n_valid, *, tm):
    n, d = h.shape
    slots = src_token.shape[0]
    grid_spec = pltpu.PrefetchScalarGridSpec(
        num_scalar_prefetch=2,
        grid=(slots // tm,),
        in_specs=[pl.BlockSpec(memory_space=pl.ANY)],
        out_specs=pl.BlockSpec((tm, d), lambda t, src, nv: (t, 0)),
        scratch_shapes=[pltpu.SemaphoreType.DMA(())],
    )
    return pl.pallas_call(
        functools.partial(_gather_kernel, tm=tm),
        grid_spec=grid_spec,
        out_shape=jax.ShapeDtypeStruct((slots, d), h.dtype),
        compiler_params=_cparams(1),
        name="gather_rows",
    )(src_token, n_valid, h)


def _combine_kernel(slot_ref, h_ref, rt_ref, ys_hbm, out_ref, buf_ref, sem, *, tm, ntok):
    base = pl.program_id(0) * tm

    def issue(r, carry):
        for kk in range(TOP_K):
            sl = slot_ref[kk * ntok + base + r]
            pltpu.make_async_copy(ys_hbm.at[pl.ds(sl, 1)],
                                  buf_ref.at[kk, pl.ds(r, 1)], sem).start()
        return carry

    lax.fori_loop(0, tm, issue, 0)
    for kk in range(TOP_K):
        pltpu.make_async_copy(ys_hbm.at[pl.ds(0, tm)], buf_ref.at[kk], sem).wait()
    rt = rt_ref[...]
    out_ref[...] = h_ref[...] + rt[:, 2:3] * buf_ref[0] + rt[:, 3:4] * buf_ref[1]


def _combine(h, routing, ys, slot_of, *, tm):
    n, d = h.shape
    grid_spec = pltpu.PrefetchScalarGridSpec(
        num_scalar_prefetch=1,
        grid=(n // tm,),
        in_specs=[
            pl.BlockSpec((tm, d), lambda i, sl: (i, 0)),
            pl.BlockSpec((tm, LANES), lambda i, sl: (i, 0)),
            pl.BlockSpec(memory_space=pl.ANY),
        ],
        out_specs=pl.BlockSpec((tm, d), lambda i, sl: (i, 0)),
        scratch_shapes=[pltpu.VMEM((TOP_K, tm, d), F32), pltpu.SemaphoreType.DMA(())],
    )
    return pl.pallas_call(
        functools.partial(_combine_kernel, tm=tm, ntok=n),
        grid_spec=grid_spec,
        out_shape=jax.ShapeDtypeStruct((n, d), F32),
        compiler_params=_cparams(1),
        name="combine",
    )(slot_of, h, routing, ys)


def _moe(h, g, router_w, router_b, wg, wu, wd):
    n, d = h.shape
    nexp = router_w.shape[1]
    tm = min(MOE_TM, n)
    routing = _router(h, g, router_w, router_b)
    experts = routing[:, :TOP_K].astype(jnp.int32).T.reshape(-1)
    onehot = (experts[:, None] == jnp.arange(nexp, dtype=jnp.int32)[None, :]).astype(jnp.int32)
    rank = jnp.sum((jnp.cumsum(onehot, axis=0) - onehot) * onehot, axis=1)
    counts = jnp.sum(onehot, axis=0)
    tiles_per = (counts + tm - 1) // tm
    tile_end = jnp.cumsum(tiles_per)
    offs = (tile_end - tiles_per) * tm
    slot_of = (offs[experts] + rank).astype(jnp.int32)
    max_tiles = (TOP_K * n) // tm + nexp
    tokens = jnp.tile(jnp.arange(n, dtype=jnp.int32), TOP_K)
    src_token = jnp.zeros((max_tiles * tm,), jnp.int32).at[slot_of].set(tokens)
    n_valid = tile_end[-1:].astype(jnp.int32)
    tile_expert = jnp.minimum(
        jnp.searchsorted(tile_end, jnp.arange(max_tiles, dtype=jnp.int32), side="right"),
        nexp - 1).astype(jnp.int32)
    xs = _gather_rows(h, src_token, n_valid, tm=tm)
    ys = _grouped_ffn(xs, g, wg, wu, wd, tile_expert, n_valid, tm=tm)
    return _combine(h, routing, ys, slot_of, tm=min(COMBINE_TM, n))


def _norm_kernel(x_ref, g_ref, out_ref):
    out_ref[...] = _rms(x_ref[...], g_ref[...])


def _final_norm(h, g):
    n, d = h.shape
    tm = min(NORM_TM, n)
    return pl.pallas_call(
        _norm_kernel,
        grid=(n // tm,),
        in_specs=[pl.BlockSpec((tm, d), lambda i: (i, 0)),
                  pl.BlockSpec((1, d), lambda i: (0, 0))],
        out_specs=pl.BlockSpec((tm, d), lambda i: (i, 0)),
        out_shape=jax.ShapeDtypeStruct((n, d), F32),
        compiler_params=_cparams(1),
        name="final_norm",
    )(h, g.reshape(1, d))


def kernel(x, norm1_g, norm2_g, final_g, w_in, mlstm_gate_b, conv_w, mlstm_norm_g, pool_w, pool_scale, w_branch_m, w_branch_p, w_out, ffn_w_gate, ffn_w_up, ffn_w_down, router_w, router_b, exp_w_gate, exp_w_up, exp_w_down):
    batch, seq, d = x.shape
    depth = w_in.shape[0]
    nheads = mlstm_gate_b.shape[1] // 2
    dm = conv_w.shape[2] // 2
    dh = dm // nheads
    dp = pool_scale.shape[1]
    assert dm % d == 0 or d % dm == 0
    col_o_end = 4 * dm
    col_p = col_o_end + 2 * nheads
    col_g = col_p + dp
    w_main = jnp.concatenate(
        [w_in[:, :, :col_o_end], w_in[:, :, col_g:], w_in[:, :, col_p:col_g]], axis=-1).astype(BF16)
    w_if = jnp.zeros((depth, d, LANES), F32).at[:, :, :2 * nheads].set(
        w_in[:, :, col_o_end:col_p]).astype(BF16)
    gate_block = col_o_end // d
    pool_block = (col_o_end + 2 * d) // dp
    wbm = w_branch_m.astype(BF16)
    wbp = w_branch_p.astype(BF16)
    wo = w_out.astype(BF16)
    pw = pool_w.astype(BF16)

    h = x.reshape(batch * seq, d)
    for l in range(depth):
        z, gates = _inproj(h, norm1_g[l], w_main[l], w_if[l])
        hm = _mlstm(z, gates, mlstm_gate_b[l], conv_w[l], mlstm_norm_g[l],
                    batch=batch, seq=seq, nheads=nheads, dh=dh)
        hp = _pool(z, pw[l], pool_scale[l], seq=seq, col_block=pool_block)
        h = _merge(h, hm, hp, z, wbm[l], wbp[l], wo[l], gate_block=gate_block)
        j = l // 2
        if l % 2 == 0:
            h = _ffn(h, norm2_g[l], ffn_w_gate[j], ffn_w_up[j], ffn_w_down[j])
        else:
            h = _moe(h, norm2_g[l], router_w[j], router_b[j],
                     exp_w_gate[j], exp_w_up[j], exp_w_down[j])
    return _final_norm(h, final_g).reshape(batch, seq, d)
```

```python
import functools

import jax
import jax.numpy as jnp
from jax import lax
from jax.experimental import pallas as pl
from jax.experimental.pallas import tpu as pltpu

F32 = jnp.float32
BF16 = jnp.bfloat16
I32 = jnp.int32

EPS = 1e-6
CONV_K = 4
POOL_WINDOWS = (2, 4, 8, 16)
POOL_HALO = 16
TOP_K = 2
LANES = 128
SUBLANES = 8
NEG_BIG = -1e30
HI16 = -65536
VMEM_LIMIT = 56 * 1024 * 1024

INPROJ_TM = 1024
INPROJ_TN = 1024
MLSTM_T = 512
MLSTM_CHUNK = 256
POOL_T = 512
MERGE_TM = 512
FFN_TM = 1024
FFN_TF = 256
MOE_TM = 1024
COMBINE_TM = 256
NORM_TM = 1024
DMA_UNROLL = 8


def _cparams(n_axes):
    return pltpu.CompilerParams(
        dimension_semantics=("arbitrary",) * n_axes,
        vmem_limit_bytes=VMEM_LIMIT,
    )


def _rms(x, g):
    ms = jnp.mean(x * x, axis=-1, keepdims=True)
    return x * lax.rsqrt(ms + EPS) * g


def _sigmoid(x):
    return 1.0 / (1.0 + jnp.exp(-x))


def _inproj_kernel(x_ref, g_ref, w_ref, wif_ref, cw_ref, z_ref, gates_ref, xn_ref, halo_ref,
                   *, tiles_per_seq, n_q, n_qk, n_qkvo, k_scale):
    i = pl.program_id(0)
    j = pl.program_id(1)
    tm = x_ref.shape[0]
    n_v_end = n_qk + (n_qk - n_q)

    @pl.when(jnp.logical_and(i == 0, j == 0))
    def _():
        halo_ref[...] = jnp.zeros_like(halo_ref)

    @pl.when(j == 0)
    def _():
        xn = _rms(x_ref[...], g_ref[...]).astype(BF16)
        xn_ref[...] = xn
        gates_ref[...] = jnp.dot(xn, wif_ref[...], preferred_element_type=F32)

    def project():
        return jnp.dot(xn_ref[...], w_ref[...], preferred_element_type=F32)

    @pl.when(j < n_qk)
    def _():
        acc = project()
        halo = jnp.where(i % tiles_per_seq == 0, 0.0, halo_ref[j])
        ext = jnp.concatenate([halo, acc], axis=0)
        w = cw_ref[...]
        y = w[CONV_K - 1:CONV_K] * acc
        for d in range(1, CONV_K):
            y = y + w[CONV_K - 1 - d:CONV_K - d] * pltpu.roll(ext, d, axis=0)[SUBLANES:]
        halo_ref[j] = acc[tm - SUBLANES:]
        y = y * _sigmoid(y) * jnp.where(j >= n_q, k_scale, 1.0)
        z_ref[...] = y.astype(BF16)

    @pl.when(jnp.logical_and(j >= n_v_end, j < n_qkvo))
    def _():
        z_ref[...] = _sigmoid(project()).astype(BF16)

    @pl.when(jnp.logical_or(jnp.logical_and(j >= n_qk, j < n_v_end), j >= n_qkvo))
    def _():
        z_ref[...] = project().astype(BF16)


def _inproj(h, g, w_main, w_if, conv_w, layer, *, seq, dm, dh):
    n, d = h.shape
    cols = w_main.shape[2]
    tm = min(INPROJ_TM, n, seq)
    tn = min(INPROJ_TN, dm)
    assert seq % tm == 0 and dm % tn == 0 and cols % tn == 0
    n_q = dm // tn
    cw = jnp.zeros((SUBLANES, 2 * dm), F32).at[:CONV_K].set(conv_w)
    return pl.pallas_call(
        functools.partial(_inproj_kernel, tiles_per_seq=seq // tm, n_q=n_q, n_qk=2 * n_q,
                          n_qkvo=4 * n_q, k_scale=dh ** -0.5),
        grid=(n // tm, cols // tn),
        in_specs=[
            pl.BlockSpec((tm, d), lambda i, j: (i, 0)),
            pl.BlockSpec((1, d), lambda i, j: (0, 0)),
            pl.BlockSpec((None, d, tn), lambda i, j: (layer, 0, j)),
            pl.BlockSpec((None, d, LANES), lambda i, j: (layer, 0, 0)),
            pl.BlockSpec((SUBLANES, tn), lambda i, j: (0, jnp.minimum(j, 2 * n_q - 1))),
        ],
        out_specs=[
            pl.BlockSpec((tm, tn), lambda i, j: (i, j)),
            pl.BlockSpec((tm, LANES), lambda i, j: (i, 0)),
        ],
        out_shape=[
            jax.ShapeDtypeStruct((n, cols), BF16),
            jax.ShapeDtypeStruct((n, LANES), F32),
        ],
        scratch_shapes=[pltpu.VMEM((tm, d), BF16),
                        pltpu.VMEM((2 * n_q, SUBLANES, tn), F32)],
        compiler_params=_cparams(2),
        name="inproj",
    )(h, g.reshape(1, d), w_main, w_if, cw)


def _mlstm_kernel(q_ref, k_ref, v_ref, og_ref, gt_ref, gb_ref, ng_ref,
                  out_ref, ct_ref, n_ref, m_ref, *, chunk, nheads):
    t_rows, dm = q_ref.shape
    dh = dm // nheads

    @pl.when(pl.program_id(1) == 0)
    def _():
        ct_ref[...] = jnp.zeros_like(ct_ref)
        n_ref[...] = jnp.zeros_like(n_ref)
        m_ref[...] = jnp.zeros_like(m_ref)

    gts = gt_ref[...] + gb_ref[...]
    lf = jnp.minimum(gts, 0.0) - jnp.log(1.0 + jnp.exp(-jnp.abs(gts)))
    in_chunk = lax.broadcasted_iota(I32, gts.shape, 0) & (chunk - 1)
    b = lf
    d = 1
    while d < chunk:
        b = b + jnp.where(in_chunk >= d, pltpu.roll(b, d, axis=0), 0.0)
        d *= 2
    gts_t = gts.T
    b_t = b.T

    tt = lax.broadcasted_iota(I32, (chunk, chunk), 0)
    ss = lax.broadcasted_iota(I32, (chunk, chunk), 1)
    causal = tt >= ss

    for hd in range(nheads):
        cs = slice(hd * dh, (hd + 1) * dh)
        icol = gts[:, hd:hd + 1]
        bcol = b[:, nheads + hd:nheads + hd + 1]
        irow = gts_t[hd:hd + 1, :]
        brow = b_t[nheads + hd:nheads + hd + 1, :]
        ct = ct_ref[hd]
        nvec = n_ref[hd]
        m_prev = m_ref[:, hd:hd + 1]
        ng = ng_ref[:, cs]
        for c in range(t_rows // chunk):
            lo, hi = c * chunk, (c + 1) * chunk
            qb, kb = q_ref[lo:hi, cs], k_ref[lo:hi, cs]
            qc, kc = qb.astype(F32), kb.astype(F32)
            vc = v_ref[lo:hi, cs]
            bc, ic = bcol[lo:hi], icol[lo:hi]
            br, ir = brow[:, lo:hi], irow[:, lo:hi]
            dmat = jnp.where(causal, bc - br + ir, NEG_BIG)
            inter = bc + m_prev
            mt = jnp.maximum(inter, jnp.max(dmat, axis=1, keepdims=True))
            wi = jnp.exp(inter - mt)
            s = lax.dot_general(qb, kb, (((1,), (1,)), ((), ())),
                                preferred_element_type=F32) * jnp.exp(dmat - mt)
            num = wi * jnp.dot(qb, ct.astype(BF16), preferred_element_type=F32)
            num = num + jnp.dot(s.astype(BF16), vc, preferred_element_type=F32)
            den = wi * jnp.sum(qc * nvec, axis=1, keepdims=True)
            den = den + jnp.sum(s, axis=1, keepdims=True)
            hr = num / jnp.maximum(jnp.abs(den), jnp.exp(-mt))
            hn = _rms(hr, ng)
            out_ref[lo:hi, cs] = (og_ref[lo:hi, cs].astype(F32) * hn).astype(BF16)
            bl = bc[chunk - 1:chunk]
            gcol = bl - bc + ic
            m_new = jnp.maximum(bl + m_prev, jnp.max(gcol, axis=0, keepdims=True))
            a = jnp.exp(bl + m_prev - m_new)
            kw = kc * jnp.exp(gcol - m_new)
            ct = a * ct + lax.dot_general(kw.astype(BF16), vc, (((0,), (0,)), ((), ())),
                                          preferred_element_type=F32)
            nvec = a * nvec + jnp.sum(kw, axis=0, keepdims=True)
            m_prev = m_new
        ct_ref[hd] = ct
        n_ref[hd] = nvec
        m_ref[:, hd:hd + 1] = m_prev


def _mlstm(z, gates, gate_b, norm_g, *, batch, seq, nheads, dh):
    n = z.shape[0]
    t = min(MLSTM_T, seq)
    chunk = min(MLSTM_CHUNK, t)
    spb = seq // t
    dm = nheads * dh
    gb = jnp.zeros((1, LANES), F32).at[0, :2 * nheads].set(gate_b)

    def zspec(col_block):
        return pl.BlockSpec((t, dm), lambda b, s: (b * spb + s, col_block))

    return pl.pallas_call(
        functools.partial(_mlstm_kernel, chunk=chunk, nheads=nheads),
        grid=(batch, spb),
        in_specs=[
            zspec(0), zspec(1), zspec(2), zspec(3),
            pl.BlockSpec((t, LANES), lambda b, s: (b * spb + s, 0)),
            pl.BlockSpec((1, LANES), lambda b, s: (0, 0)),
            pl.BlockSpec((1, dm), lambda b, s: (0, 0)),
        ],
        out_specs=pl.BlockSpec((t, dm), lambda b, s: (b * spb + s, 0)),
        out_shape=jax.ShapeDtypeStruct((n, dm), BF16),
        scratch_shapes=[
            pltpu.VMEM((nheads, dh, dh), F32),
            pltpu.VMEM((nheads, 1, dh), F32),
            pltpu.VMEM((1, LANES), F32),
        ],
        compiler_params=_cparams(2),
        name="mlstm",
    )(z, z, z, z, gates, gb, norm_g.reshape(1, dm))


def _pool_kernel(p_ref, halo_ref, w_ref, sc_ref, out_ref, *, tiles_per_seq):
    t_rows = p_ref.shape[0]
    grp = w_ref.shape[1]
    pos0 = (pl.program_id(0) % tiles_per_seq) * t_rows
    p = p_ref[...].astype(F32)
    halo = jnp.where(pos0 == 0, 0.0, halo_ref[...].astype(F32))
    ext = jnp.concatenate([halo, p], axis=0)
    pos = pos0 + lax.broadcasted_iota(I32, (t_rows, 1), 0)
    posf = (pos + 1).astype(F32)
    for j, w in enumerate(POOL_WINDOWS):
        cs = slice(j * grp, (j + 1) * grp)
        win = ext[:, cs]
        d = 1
        while d < w:
            win = win + pltpu.roll(win, d, axis=0)
            d *= 2
        pooled = win[POOL_HALO:] / jnp.minimum(posf, float(w)) - p[:, cs]
        y = jnp.dot(pooled.astype(BF16), w_ref[j], preferred_element_type=F32)
        out_ref[:, cs] = (y * sc_ref[:, cs]).astype(BF16)


def _pool(z, pool_w, pool_scale, *, seq, col_block):
    n = z.shape[0]
    ngrp, grp, _ = pool_w.shape
    dp = ngrp * grp
    t = min(POOL_T, seq)
    tps = seq // t
    hb = t // POOL_HALO
    return pl.pallas_call(
        functools.partial(_pool_kernel, tiles_per_seq=tps),
        grid=(n // t,),
        in_specs=[
            pl.BlockSpec((t, dp), lambda i: (i, col_block)),
            pl.BlockSpec((POOL_HALO, dp), lambda i: (jnp.maximum(i * hb - 1, 0), col_block)),
            pl.BlockSpec((ngrp, grp, grp), lambda i: (0, 0, 0)),
            pl.BlockSpec((1, dp), lambda i: (0, 0)),
        ],
        out_specs=pl.BlockSpec((t, dp), lambda i: (i, 0)),
        out_shape=jax.ShapeDtypeStruct((n, dp), BF16),
        compiler_params=_cparams(1),
        name="pool",
    )(z, z, pool_w, pool_scale.reshape(1, dp))


def _merge_kernel(h_ref, hm_ref, hp_ref, gm_ref, gp_ref, wbm_ref, wbp_ref, wo_ref, out_ref):
    ym = jnp.dot(hm_ref[...], wbm_ref[...], preferred_element_type=F32)
    yp = jnp.dot(hp_ref[...], wbp_ref[...], preferred_element_type=F32)
    y = _sigmoid(gm_ref[...].astype(F32)) * ym + _sigmoid(gp_ref[...].astype(F32)) * yp
    out_ref[...] = h_ref[...] + jnp.dot(y.astype(BF16), wo_ref[...],
                                        preferred_element_type=F32)


def _merge(h, hm, hp, z, wbm, wbp, wo, layer, *, gate_block):
    n, d = h.shape
    dm, dp = hm.shape[1], hp.shape[1]
    tm = min(MERGE_TM, n)
    once = pl.Buffered(1)
    return pl.pallas_call(
        _merge_kernel,
        grid=(n // tm,),
        in_specs=[
            pl.BlockSpec((tm, d), lambda i: (i, 0)),
            pl.BlockSpec((tm, dm), lambda i: (i, 0)),
            pl.BlockSpec((tm, dp), lambda i: (i, 0)),
            pl.BlockSpec((tm, d), lambda i: (i, gate_block)),
            pl.BlockSpec((tm, d), lambda i: (i, gate_block + 1)),
            pl.BlockSpec((None, dm, d), lambda i: (layer, 0, 0), pipeline_mode=once),
            pl.BlockSpec((None, dp, d), lambda i: (layer, 0, 0), pipeline_mode=once),
            pl.BlockSpec((None, d, d), lambda i: (layer, 0, 0), pipeline_mode=once),
        ],
        out_specs=pl.BlockSpec((tm, d), lambda i: (i, 0)),
        out_shape=jax.ShapeDtypeStruct((n, d), F32),
        compiler_params=_cparams(1),
        name="merge",
    )(h, hm, hp, z, z, wbm, wbp, wo)


def _swiglu_step(xn_ref, wg_ref, wu_ref, wd_ref, out_ref):
    xn = xn_ref[...]
    a = jnp.dot(xn, wg_ref[...].astype(BF16), preferred_element_type=F32)
    u = jnp.dot(xn, wu_ref[...].astype(BF16), preferred_element_type=F32)
    act = (a * _sigmoid(a) * u).astype(BF16)
    out_ref[...] += jnp.dot(act, wd_ref[...].astype(BF16), preferred_element_type=F32)


def _ffn_kernel(x_ref, g_ref, wg_ref, wu_ref, wd_ref, out_ref, xn_ref):
    @pl.when(pl.program_id(1) == 0)
    def _():
        x = x_ref[...]
        xn_ref[...] = _rms(x, g_ref[...]).astype(BF16)
        out_ref[...] = x

    _swiglu_step(xn_ref, wg_ref, wu_ref, wd_ref, out_ref)


def _ffn(h, g, wg, wu, wd, j):
    n, d = h.shape
    ff = wg.shape[2]
    tm = min(FFN_TM, n)
    tf = min(FFN_TF, ff)
    return pl.pallas_call(
        _ffn_kernel,
        grid=(n // tm, ff // tf),
        in_specs=[
            pl.BlockSpec((tm, d), lambda i, f: (i, 0), pipeline_mode=pl.Buffered(1)),
            pl.BlockSpec((1, d), lambda i, f: (0, 0)),
            pl.BlockSpec((None, d, tf), lambda i, f: (j, 0, f)),
            pl.BlockSpec((None, d, tf), lambda i, f: (j, 0, f)),
            pl.BlockSpec((None, tf, d), lambda i, f: (j, f, 0)),
        ],
        out_specs=pl.BlockSpec((tm, d), lambda i, f: (i, 0)),
        out_shape=jax.ShapeDtypeStruct((n, d), F32),
        scratch_shapes=[pltpu.VMEM((tm, d), BF16)],
        compiler_params=_cparams(2),
        name="ffn",
    )(h, g.reshape(1, d), wg, wu, wd)


def _router_kernel(x_ref, g_ref, wh_ref, wl_ref, rb_ref, out_ref, xp_ref, *, nexp):
    xn = _rms(x_ref[...], g_ref[...])
    xh = xn.astype(BF16)
    xhf = xh.astype(F32)
    xl = (xn - xhf).astype(BF16)
    half = xn.shape[1] // 2
    lo = lax.shift_right_logical(lax.bitcast_convert_type(xhf[:, :half], I32), 16)
    hi = lax.bitcast_convert_type(xhf[:, half:], I32) & HI16
    xp_ref[...] = hi | lo
    wh, wl = wh_ref[...], wl_ref[...]
    logits = (jnp.dot(xh, wh, preferred_element_type=F32)
              + jnp.dot(xh, wl, preferred_element_type=F32)
              + jnp.dot(xl, wh, preferred_element_type=F32)) + rb_ref[...]
    lane = lax.broadcasted_iota(I32, logits.shape, 1)
    logits = jnp.where(lane < nexp, logits, NEG_BIG)
    v1 = jnp.max(logits, axis=1, keepdims=True)
    i1 = jnp.min(jnp.where(logits == v1, lane, LANES), axis=1, keepdims=True)
    rest = jnp.where(lane == i1, NEG_BIG, logits)
    v2 = jnp.max(rest, axis=1, keepdims=True)
    i2 = jnp.min(jnp.where(rest == v2, lane, LANES), axis=1, keepdims=True)
    e2 = jnp.exp(v2 - v1)
    g1 = 1.0 / (1.0 + e2)
    g2 = e2 / (1.0 + e2)
    out = jnp.where(lane == 0, i1.astype(F32), 0.0)
    out = jnp.where(lane == 1, i2.astype(F32), out)
    out = jnp.where(lane == 2, g1, out)
    out = jnp.where(lane == 3, g2, out)
    out_ref[...] = out


def _router(h, g, router_w, router_b):
    n, d = h.shape
    nexp = router_w.shape[1]
    tm = min(NORM_TM, n)
    w = jnp.zeros((d, LANES), F32).at[:, :nexp].set(router_w)
    wh = w.astype(BF16)
    wl = (w - wh.astype(F32)).astype(BF16)
    rb = jnp.zeros((1, LANES), F32).at[0, :nexp].set(router_b)
    return pl.pallas_call(
        functools.partial(_router_kernel, nexp=nexp),
        grid=(n // tm,),
        in_specs=[
            pl.BlockSpec((tm, d), lambda i: (i, 0)),
            pl.BlockSpec((1, d), lambda i: (0, 0)),
            pl.BlockSpec((d, LANES), lambda i: (0, 0)),
            pl.BlockSpec((d, LANES), lambda i: (0, 0)),
            pl.BlockSpec((1, LANES), lambda i: (0, 0)),
        ],
        out_specs=[
            pl.BlockSpec((tm, LANES), lambda i: (i, 0)),
            pl.BlockSpec((tm, d // 2), lambda i: (i, 0)),
        ],
        out_shape=[
            jax.ShapeDtypeStruct((n, LANES), F32),
            jax.ShapeDtypeStruct((n, d // 2), I32),
        ],
        compiler_params=_cparams(1),
        name="router",
    )(h, g.reshape(1, d), wh, wl, rb)


def _gffn_kernel(te_ref, nv_ref, src_ref, xp_hbm, wg_ref, wu_ref, wd_ref, out_ref,
                 xg_ref, xn_ref, sem, *, tm):
    t = pl.program_id(0)
    f = pl.program_id(1)
    nv = nv_ref[0]
    half = xg_ref.shape[2]

    def issue(tile, slot):
        base = tile * tm

        def body(r, carry):
            tok = src_ref[base + r]
            pltpu.make_async_copy(xp_hbm.at[pl.ds(tok, 1)],
                                  xg_ref.at[slot, pl.ds(r, 1)], sem.at[slot]).start()
            return carry

        lax.fori_loop(0, tm, body, 0, unroll=DMA_UNROLL)

    @pl.when(t < nv)
    def _():
        @pl.when(f == 0)
        def _():
            slot = t % 2

            @pl.when(t == 0)
            def _():
                issue(0, 0)

            pltpu.make_async_copy(xp_hbm.at[pl.ds(0, tm)], xg_ref.at[slot], sem.at[slot]).wait()

            @pl.when(t + 1 < nv)
            def _():
                issue(t + 1, 1 - slot)

            p = xg_ref[slot]
            xn_ref[:, :half] = lax.bitcast_convert_type(lax.shift_left(p, 16), F32).astype(BF16)
            xn_ref[:, half:] = lax.bitcast_convert_type(p & HI16, F32).astype(BF16)
            out_ref[...] = jnp.zeros_like(out_ref)

        _swiglu_step(xn_ref, wg_ref, wu_ref, wd_ref, out_ref)

    @pl.when(jnp.logical_and(t >= nv, f == 0))
    def _():
        out_ref[...] = jnp.zeros_like(out_ref)


def _grouped_ffn(xp, wg, wu, wd, j, tile_expert, n_valid, src_token, *, tm):
    half = xp.shape[1]
    d = 2 * half
    slots = src_token.shape[0]
    ff = wg.shape[3]
    tf = min(FFN_TF, ff)
    nf = ff // tf
    ntiles = slots // tm

    def fcol(t, f, nv):
        return jnp.where(t < nv[0], f, nf - 1)

    grid_spec = pltpu.PrefetchScalarGridSpec(
        num_scalar_prefetch=3,
        grid=(ntiles, nf),
        in_specs=[
            pl.BlockSpec(memory_space=pl.ANY),
            pl.BlockSpec((None, None, d, tf),
                         lambda t, f, te, nv, src: (j, te[t], 0, fcol(t, f, nv))),
            pl.BlockSpec((None, None, d, tf),
                         lambda t, f, te, nv, src: (j, te[t], 0, fcol(t, f, nv))),
            pl.BlockSpec((None, None, tf, d),
                         lambda t, f, te, nv, src: (j, te[t], fcol(t, f, nv), 0)),
        ],
        out_specs=pl.BlockSpec((tm, d), lambda t, f, te, nv, src: (t, 0)),
        scratch_shapes=[
            pltpu.VMEM((2, tm, half), I32),
            pltpu.VMEM((tm, d), BF16),
            pltpu.SemaphoreType.DMA((2,)),
        ],
    )
    return pl.pallas_call(
        functools.partial(_gffn_kernel, tm=tm),
        grid_spec=grid_spec,
        out_shape=jax.ShapeDtypeStruct((slots, d), F32),
        compiler_params=_cparams(2),
        name="grouped_ffn",
    )(tile_expert, n_valid, src_token, xp, wg, wu, wd)


def _combine_kernel(slot_ref, h_ref, rt_ref, ys_hbm, out_ref, buf_ref, sem, *, tm, ntok):
    base = pl.program_id(0) * tm

    def issue(r, carry):
        for kk in range(TOP_K):
            sl = slot_ref[kk * ntok + base + r]
            pltpu.make_async_copy(ys_hbm.at[pl.ds(sl, 1)],
                                  buf_ref.at[kk, pl.ds(r, 1)], sem).start()
        return carry

    lax.fori_loop(0, tm, issue, 0, unroll=DMA_UNROLL)
    for kk in range(TOP_K):
        pltpu.make_async_copy(ys_hbm.at[pl.ds(0, tm)], buf_ref.at[kk], sem).wait()
    rt = rt_ref[...]
    out_ref[...] = h_ref[...] + rt[:, 2:3] * buf_ref[0] + rt[:, 3:4] * buf_ref[1]


def _combine(h, routing, ys, slot_of, *, tm):
    n, d = h.shape
    grid_spec = pltpu.PrefetchScalarGridSpec(
        num_scalar_prefetch=1,
        grid=(n // tm,),
        in_specs=[
            pl.BlockSpec((tm, d), lambda i, sl: (i, 0)),
            pl.BlockSpec((tm, LANES), lambda i, sl: (i, 0)),
            pl.BlockSpec(memory_space=pl.ANY),
        ],
        out_specs=pl.BlockSpec((tm, d), lambda i, sl: (i, 0)),
        scratch_shapes=[pltpu.VMEM((TOP_K, tm, d), F32), pltpu.SemaphoreType.DMA(())],
    )
    return pl.pallas_call(
        functools.partial(_combine_kernel, tm=tm, ntok=n),
        grid_spec=grid_spec,
        out_shape=jax.ShapeDtypeStruct((n, d), F32),
        compiler_params=_cparams(1),
        name="combine",
    )(slot_of, h, routing, ys)


def _moe(h, g, router_w, router_b, wg, wu, wd, j):
    n, d = h.shape
    nexp = router_w.shape[1]
    tm = min(MOE_TM, n)
    routing, xp = _router(h, g, router_w, router_b)
    experts = routing[:, :TOP_K].astype(I32).T.reshape(-1)
    onehot = (experts[:, None] == jnp.arange(nexp, dtype=I32)[None, :]).astype(I32)
    rank = jnp.sum((jnp.cumsum(onehot, axis=0) - onehot) * onehot, axis=1)
    counts = jnp.sum(onehot, axis=0)
    tiles_per = (counts + tm - 1) // tm
    tile_end = jnp.cumsum(tiles_per)
    offs = (tile_end - tiles_per) * tm
    slot_of = (offs[experts] + rank).astype(I32)
    max_tiles = (TOP_K * n) // tm + nexp
    tokens = jnp.tile(jnp.arange(n, dtype=I32), TOP_K)
    src_token = jnp.zeros((max_tiles * tm,), I32).at[slot_of].set(tokens)
    n_valid = tile_end[-1:].astype(I32)
    tile_ids = jnp.arange(max_tiles, dtype=I32)
    tile_expert = jnp.minimum(
        jnp.sum((tile_ids[:, None] >= tile_end[None, :]).astype(I32), axis=1), nexp - 1)
    ys = _grouped_ffn(xp, wg, wu, wd, j, tile_expert, n_valid, src_token, tm=tm)
    return _combine(h, routing, ys, slot_of, tm=min(COMBINE_TM, n))


def _norm_kernel(x_ref, g_ref, out_ref):
    out_ref[...] = _rms(x_ref[...], g_ref[...])


def _final_norm(h, g):
    n, d = h.shape
    tm = min(NORM_TM, n)
    return pl.pallas_call(
        _norm_kernel,
        grid=(n // tm,),
        in_specs=[pl.BlockSpec((tm, d), lambda i: (i, 0)),
                  pl.BlockSpec((1, d), lambda i: (0, 0))],
        out_specs=pl.BlockSpec((tm, d), lambda i: (i, 0)),
        out_shape=jax.ShapeDtypeStruct((n, d), F32),
        compiler_params=_cparams(1),
        name="final_norm",
    )(h, g.reshape(1, d))


def kernel(x, norm1_g, norm2_g, final_g, w_in, mlstm_gate_b, conv_w, mlstm_norm_g, pool_w, pool_scale, w_branch_m, w_branch_p, w_out, ffn_w_gate, ffn_w_up, ffn_w_down, router_w, router_b, exp_w_gate, exp_w_up, exp_w_down):
    batch, seq, d = x.shape
    depth = w_in.shape[0]
    nheads = mlstm_gate_b.shape[1] // 2
    dm = conv_w.shape[2] // 2
    dh = dm // nheads
    dp = pool_scale.shape[1]
    col_o_end = 4 * dm
    col_p = col_o_end + 2 * nheads
    col_g = col_p + dp
    assert col_o_end % d == 0 and (col_o_end + 2 * d) % dp == 0
    w_main = jnp.concatenate(
        [w_in[:, :, :col_o_end], w_in[:, :, col_g:], w_in[:, :, col_p:col_g]], axis=-1).astype(BF16)
    w_if = jnp.zeros((depth, d, LANES), F32).at[:, :, :2 * nheads].set(
        w_in[:, :, col_o_end:col_p]).astype(BF16)
    gate_block = col_o_end // d
    pool_block = (col_o_end + 2 * d) // dp
    wbm = w_branch_m.astype(BF16)
    wbp = w_branch_p.astype(BF16)
    wo = w_out.astype(BF16)
    pw = pool_w.astype(BF16)

    h = x.reshape(batch * seq, d)
    for l in range(depth):
        z, gates = _inproj(h, norm1_g[l], w_main, w_if, conv_w[l], l, seq=seq, dm=dm, dh=dh)
        hm = _mlstm(z, gates, mlstm_gate_b[l], mlstm_norm_g[l],
                    batch=batch, seq=seq, nheads=nheads, dh=dh)
        hp = _pool(z, pw[l], pool_scale[l], seq=seq, col_block=pool_block)
        h = _merge(h, hm, hp, z, wbm, wbp, wo, l, gate_block=gate_block)
        j = l // 2
        if l % 2 == 0:
            h = _ffn(h, norm2_g[l], ffn_w_gate, ffn_w_up, ffn_w_down, j)
        else:
            h = _moe(h, norm2_g[l], router_w[j], router_b[j],
                     exp_w_gate, exp_w_up, exp_w_down, j)
    return _final_norm(h, final_g).reshape(batch, seq, d)
```

```python
import functools

import jax
import jax.numpy as jnp
from jax import lax
from jax.experimental import pallas as pl
from jax.experimental.pallas import tpu as pltpu

F32 = jnp.float32
BF16 = jnp.bfloat16
I32 = jnp.int32

EPS = 1e-6
CONV_K = 4
POOL_WINDOWS = (2, 4, 8, 16)
POOL_HALO = 16
TOP_K = 2
LANES = 128
SUBLANES = 8
NEG_BIG = -1e30
HI16 = -65536
VMEM_LIMIT = 56 * 1024 * 1024

INPROJ_TM = 1024
INPROJ_TN = 1024
MLSTM_T = 512
MLSTM_CHUNK = 256
POOL_T = 512
MERGE_TM = 512
FFN_TM = 1024
FFN_TF = 256
MOE_TM = 1024
MOE_SUB = 256
COMBINE_TM = 256
NORM_TM = 1024
DMA_UNROLL = 8


def _cparams(n_axes):
    return pltpu.CompilerParams(
        dimension_semantics=("arbitrary",) * n_axes,
        vmem_limit_bytes=VMEM_LIMIT,
    )


def _rms(x, g):
    ms = jnp.mean(x * x, axis=-1, keepdims=True)
    return x * lax.rsqrt(ms + EPS) * g


def _sigmoid(x):
    return 1.0 / (1.0 + jnp.exp(-x))


def _inproj_kernel(x_ref, g_ref, w_ref, wif_ref, cw_ref, z_ref, gates_ref, xn_ref, halo_ref,
                   *, tiles_per_seq, n_q, n_qk, n_qkvo, k_scale):
    i = pl.program_id(0)
    j = pl.program_id(1)
    tm = x_ref.shape[0]
    n_v_end = n_qk + (n_qk - n_q)

    @pl.when(jnp.logical_and(i == 0, j == 0))
    def _():
        halo_ref[...] = jnp.zeros_like(halo_ref)

    @pl.when(j == 0)
    def _():
        xn = _rms(x_ref[...], g_ref[...]).astype(BF16)
        xn_ref[...] = xn
        gates_ref[...] = jnp.dot(xn, wif_ref[...], preferred_element_type=F32)

    def project():
        return jnp.dot(xn_ref[...], w_ref[...], preferred_element_type=F32)

    @pl.when(j < n_qk)
    def _():
        acc = project()
        halo = jnp.where(i % tiles_per_seq == 0, 0.0, halo_ref[j])
        ext = jnp.concatenate([halo, acc], axis=0)
        w = cw_ref[...]
        y = w[CONV_K - 1:CONV_K] * acc
        for d in range(1, CONV_K):
            y = y + w[CONV_K - 1 - d:CONV_K - d] * pltpu.roll(ext, d, axis=0)[SUBLANES:]
        halo_ref[j] = acc[tm - SUBLANES:]
        y = y * _sigmoid(y) * jnp.where(j >= n_q, k_scale, 1.0)
        z_ref[...] = y.astype(BF16)

    @pl.when(jnp.logical_and(j >= n_v_end, j < n_qkvo))
    def _():
        z_ref[...] = _sigmoid(project()).astype(BF16)

    @pl.when(jnp.logical_or(jnp.logical_and(j >= n_qk, j < n_v_end), j >= n_qkvo))
    def _():
        z_ref[...] = project().astype(BF16)


def _inproj(h, g, w_main, w_if, conv_w, layer, *, seq, dm, dh):
    n, d = h.shape
    cols = w_main.shape[2]
    tm = min(INPROJ_TM, n, seq)
    tn = min(INPROJ_TN, dm)
    assert seq % tm == 0 and dm % tn == 0 and cols % tn == 0
    n_q = dm // tn
    cw = jnp.zeros((SUBLANES, 2 * dm), F32).at[:CONV_K].set(conv_w)
    return pl.pallas_call(
        functools.partial(_inproj_kernel, tiles_per_seq=seq // tm, n_q=n_q, n_qk=2 * n_q,
                          n_qkvo=4 * n_q, k_scale=dh ** -0.5),
        grid=(n // tm, cols // tn),
        in_specs=[
            pl.BlockSpec((tm, d), lambda i, j: (i, 0)),
            pl.BlockSpec((1, d), lambda i, j: (0, 0)),
            pl.BlockSpec((None, d, tn), lambda i, j: (layer, 0, j)),
            pl.BlockSpec((None, d, LANES), lambda i, j: (layer, 0, 0)),
            pl.BlockSpec((SUBLANES, tn), lambda i, j: (0, jnp.minimum(j, 2 * n_q - 1))),
        ],
        out_specs=[
            pl.BlockSpec((tm, tn), lambda i, j: (i, j)),
            pl.BlockSpec((tm, LANES), lambda i, j: (i, 0)),
        ],
        out_shape=[
            jax.ShapeDtypeStruct((n, cols), BF16),
            jax.ShapeDtypeStruct((n, LANES), F32),
        ],
        scratch_shapes=[pltpu.VMEM((tm, d), BF16),
                        pltpu.VMEM((2 * n_q, SUBLANES, tn), F32)],
        compiler_params=_cparams(2),
        name="inproj",
    )(h, g.reshape(1, d), w_main, w_if, cw)


def _mlstm_kernel(q_ref, k_ref, v_ref, og_ref, gt_ref, gb_ref, ng_ref,
                  out_ref, ct_ref, n_ref, m_ref, *, chunk, nheads):
    t_rows, dm = q_ref.shape
    dh = dm // nheads

    @pl.when(pl.program_id(1) == 0)
    def _():
        ct_ref[...] = jnp.zeros_like(ct_ref)
        n_ref[...] = jnp.zeros_like(n_ref)
        m_ref[...] = jnp.zeros_like(m_ref)

    gts = gt_ref[...] + gb_ref[...]
    lf = jnp.minimum(gts, 0.0) - jnp.log(1.0 + jnp.exp(-jnp.abs(gts)))
    in_chunk = lax.broadcasted_iota(I32, gts.shape, 0) & (chunk - 1)
    b = lf
    d = 1
    while d < chunk:
        b = b + jnp.where(in_chunk >= d, pltpu.roll(b, d, axis=0), 0.0)
        d *= 2
    gts_t = gts.T
    b_t = b.T

    tt = lax.broadcasted_iota(I32, (chunk, chunk), 0)
    ss = lax.broadcasted_iota(I32, (chunk, chunk), 1)
    causal = tt >= ss

    for hd in range(nheads):
        cs = slice(hd * dh, (hd + 1) * dh)
        icol = gts[:, hd:hd + 1]
        bcol = b[:, nheads + hd:nheads + hd + 1]
        irow = gts_t[hd:hd + 1, :]
        brow = b_t[nheads + hd:nheads + hd + 1, :]
        ct = ct_ref[hd]
        nvec = n_ref[hd]
        m_prev = m_ref[:, hd:hd + 1]
        ng = ng_ref[:, cs]
        for c in range(t_rows // chunk):
            lo, hi = c * chunk, (c + 1) * chunk
            qb, kb = q_ref[lo:hi, cs], k_ref[lo:hi, cs]
            qc, kc = qb.astype(F32), kb.astype(F32)
            vc = v_ref[lo:hi, cs]
            bc, ic = bcol[lo:hi], icol[lo:hi]
            br, ir = brow[:, lo:hi], irow[:, lo:hi]
            dmat = jnp.where(causal, bc - br + ir, NEG_BIG)
            inter = bc + m_prev
            mt = jnp.maximum(inter, jnp.max(dmat, axis=1, keepdims=True))
            wi = jnp.exp(inter - mt)
            s = lax.dot_general(qb, kb, (((1,), (1,)), ((), ())),
                                preferred_element_type=F32) * jnp.exp(dmat - mt)
            num = wi * jnp.dot(qb, ct.astype(BF16), preferred_element_type=F32)
            num = num + jnp.dot(s.astype(BF16), vc, preferred_element_type=F32)
            den = wi * jnp.sum(qc * nvec, axis=1, keepdims=True)
            den = den + jnp.sum(s, axis=1, keepdims=True)
            hr = num / jnp.maximum(jnp.abs(den), jnp.exp(-mt))
            hn = _rms(hr, ng)
            out_ref[lo:hi, cs] = (og_ref[lo:hi, cs].astype(F32) * hn).astype(BF16)
            bl = bc[chunk - 1:chunk]
            gcol = bl - bc + ic
            m_new = jnp.maximum(bl + m_prev, jnp.max(gcol, axis=0, keepdims=True))
            a = jnp.exp(bl + m_prev - m_new)
            kw = kc * jnp.exp(gcol - m_new)
            ct = a * ct + lax.dot_general(kw.astype(BF16), vc, (((0,), (0,)), ((), ())),
                                          preferred_element_type=F32)
            nvec = a * nvec + jnp.sum(kw, axis=0, keepdims=True)
            m_prev = m_new
        ct_ref[hd] = ct
        n_ref[hd] = nvec
        m_ref[:, hd:hd + 1] = m_prev


def _mlstm(z, gates, gate_b, norm_g, *, batch, seq, nheads, dh):
    n = z.shape[0]
    t = min(MLSTM_T, seq)
    chunk = min(MLSTM_CHUNK, t)
    spb = seq // t
    dm = nheads * dh
    gb = jnp.zeros((1, LANES), F32).at[0, :2 * nheads].set(gate_b)

    def zspec(col_block):
        return pl.BlockSpec((t, dm), lambda b, s: (b * spb + s, col_block))

    return pl.pallas_call(
        functools.partial(_mlstm_kernel, chunk=chunk, nheads=nheads),
        grid=(batch, spb),
        in_specs=[
            zspec(0), zspec(1), zspec(2), zspec(3),
            pl.BlockSpec((t, LANES), lambda b, s: (b * spb + s, 0)),
            pl.BlockSpec((1, LANES), lambda b, s: (0, 0)),
            pl.BlockSpec((1, dm), lambda b, s: (0, 0)),
        ],
        out_specs=pl.BlockSpec((t, dm), lambda b, s: (b * spb + s, 0)),
        out_shape=jax.ShapeDtypeStruct((n, dm), BF16),
        scratch_shapes=[
            pltpu.VMEM((nheads, dh, dh), F32),
            pltpu.VMEM((nheads, 1, dh), F32),
            pltpu.VMEM((1, LANES), F32),
        ],
        compiler_params=_cparams(2),
        name="mlstm",
    )(z, z, z, z, gates, gb, norm_g.reshape(1, dm))


def _pool_kernel(p_ref, halo_ref, w_ref, sc_ref, out_ref, *, tiles_per_seq):
    t_rows = p_ref.shape[0]
    grp = w_ref.shape[1]
    pos0 = (pl.program_id(0) % tiles_per_seq) * t_rows
    p = p_ref[...].astype(F32)
    halo = jnp.where(pos0 == 0, 0.0, halo_ref[...].astype(F32))
    ext = jnp.concatenate([halo, p], axis=0)
    pos = pos0 + lax.broadcasted_iota(I32, (t_rows, 1), 0)
    posf = (pos + 1).astype(F32)
    for j, w in enumerate(POOL_WINDOWS):
        cs = slice(j * grp, (j + 1) * grp)
        win = ext[:, cs]
        d = 1
        while d < w:
            win = win + pltpu.roll(win, d, axis=0)
            d *= 2
        pooled = win[POOL_HALO:] / jnp.minimum(posf, float(w)) - p[:, cs]
        y = jnp.dot(pooled.astype(BF16), w_ref[j], preferred_element_type=F32)
        out_ref[:, cs] = (y * sc_ref[:, cs]).astype(BF16)


def _pool(z, pool_w, pool_scale, *, seq, col_block):
    n = z.shape[0]
    ngrp, grp, _ = pool_w.shape
    dp = ngrp * grp
    t = min(POOL_T, seq)
    tps = seq // t
    hb = t // POOL_HALO
    return pl.pallas_call(
        functools.partial(_pool_kernel, tiles_per_seq=tps),
        grid=(n // t,),
        in_specs=[
            pl.BlockSpec((t, dp), lambda i: (i, col_block)),
            pl.BlockSpec((POOL_HALO, dp), lambda i: (jnp.maximum(i * hb - 1, 0), col_block)),
            pl.BlockSpec((ngrp, grp, grp), lambda i: (0, 0, 0)),
            pl.BlockSpec((1, dp), lambda i: (0, 0)),
        ],
        out_specs=pl.BlockSpec((t, dp), lambda i: (i, 0)),
        out_shape=jax.ShapeDtypeStruct((n, dp), BF16),
        compiler_params=_cparams(1),
        name="pool",
    )(z, z, pool_w, pool_scale.reshape(1, dp))


def _merge_kernel(h_ref, hm_ref, hp_ref, gm_ref, gp_ref, wbm_ref, wbp_ref, wo_ref, out_ref):
    ym = jnp.dot(hm_ref[...], wbm_ref[...], preferred_element_type=F32)
    yp = jnp.dot(hp_ref[...], wbp_ref[...], preferred_element_type=F32)
    y = _sigmoid(gm_ref[...].astype(F32)) * ym + _sigmoid(gp_ref[...].astype(F32)) * yp
    out_ref[...] = h_ref[...] + jnp.dot(y.astype(BF16), wo_ref[...],
                                        preferred_element_type=F32)


def _merge(h, hm, hp, z, wbm, wbp, wo, layer, *, gate_block):
    n, d = h.shape
    dm, dp = hm.shape[1], hp.shape[1]
    tm = min(MERGE_TM, n)
    once = pl.Buffered(1)
    return pl.pallas_call(
        _merge_kernel,
        grid=(n // tm,),
        in_specs=[
            pl.BlockSpec((tm, d), lambda i: (i, 0)),
            pl.BlockSpec((tm, dm), lambda i: (i, 0)),
            pl.BlockSpec((tm, dp), lambda i: (i, 0)),
            pl.BlockSpec((tm, d), lambda i: (i, gate_block)),
            pl.BlockSpec((tm, d), lambda i: (i, gate_block + 1)),
            pl.BlockSpec((None, dm, d), lambda i: (layer, 0, 0), pipeline_mode=once),
            pl.BlockSpec((None, dp, d), lambda i: (layer, 0, 0), pipeline_mode=once),
            pl.BlockSpec((None, d, d), lambda i: (layer, 0, 0), pipeline_mode=once),
        ],
        out_specs=pl.BlockSpec((tm, d), lambda i: (i, 0)),
        out_shape=jax.ShapeDtypeStruct((n, d), F32),
        compiler_params=_cparams(1),
        name="merge",
    )(h, hm, hp, z, z, wbm, wbp, wo)


def _swiglu_step(xn_ref, wg_ref, wu_ref, wd_ref, out_ref, rows=slice(None)):
    xn = xn_ref[rows, :]
    a = jnp.dot(xn, wg_ref[...].astype(BF16), preferred_element_type=F32)
    u = jnp.dot(xn, wu_ref[...].astype(BF16), preferred_element_type=F32)
    act = (a * _sigmoid(a) * u).astype(BF16)
    out_ref[rows, :] += jnp.dot(act, wd_ref[...].astype(BF16), preferred_element_type=F32)


def _ffn_kernel(x_ref, g_ref, wg_ref, wu_ref, wd_ref, out_ref, xn_ref):
    @pl.when(pl.program_id(1) == 0)
    def _():
        x = x_ref[...]
        xn_ref[...] = _rms(x, g_ref[...]).astype(BF16)
        out_ref[...] = x

    _swiglu_step(xn_ref, wg_ref, wu_ref, wd_ref, out_ref)


def _ffn(h, g, wg, wu, wd, j):
    n, d = h.shape
    ff = wg.shape[2]
    tm = min(FFN_TM, n)
    tf = min(FFN_TF, ff)
    return pl.pallas_call(
        _ffn_kernel,
        grid=(n // tm, ff // tf),
        in_specs=[
            pl.BlockSpec((tm, d), lambda i, f: (i, 0), pipeline_mode=pl.Buffered(1)),
            pl.BlockSpec((1, d), lambda i, f: (0, 0)),
            pl.BlockSpec((None, d, tf), lambda i, f: (j, 0, f)),
            pl.BlockSpec((None, d, tf), lambda i, f: (j, 0, f)),
            pl.BlockSpec((None, tf, d), lambda i, f: (j, f, 0)),
        ],
        out_specs=pl.BlockSpec((tm, d), lambda i, f: (i, 0)),
        out_shape=jax.ShapeDtypeStruct((n, d), F32),
        scratch_shapes=[pltpu.VMEM((tm, d), BF16)],
        compiler_params=_cparams(2),
        name="ffn",
    )(h, g.reshape(1, d), wg, wu, wd)


def _router_kernel(x_ref, g_ref, wh_ref, wl_ref, rb_ref, out_ref, xp_ref, *, nexp):
    xn = _rms(x_ref[...], g_ref[...])
    xh = xn.astype(BF16)
    xhf = xh.astype(F32)
    xl = (xn - xhf).astype(BF16)
    half = xn.shape[1] // 2
    lo = lax.shift_right_logical(lax.bitcast_convert_type(xhf[:, :half], I32), 16)
    hi = lax.bitcast_convert_type(xhf[:, half:], I32) & HI16
    xp_ref[...] = hi | lo
    wh, wl = wh_ref[...], wl_ref[...]
    logits = (jnp.dot(xh, wh, preferred_element_type=F32)
              + jnp.dot(xh, wl, preferred_element_type=F32)
              + jnp.dot(xl, wh, preferred_element_type=F32)) + rb_ref[...]
    lane = lax.broadcasted_iota(I32, logits.shape, 1)
    logits = jnp.where(lane < nexp, logits, NEG_BIG)
    v1 = jnp.max(logits, axis=1, keepdims=True)
    i1 = jnp.min(jnp.where(logits == v1, lane, LANES), axis=1, keepdims=True)
    rest = jnp.where(lane == i1, NEG_BIG, logits)
    v2 = jnp.max(rest, axis=1, keepdims=True)
    i2 = jnp.min(jnp.where(rest == v2, lane, LANES), axis=1, keepdims=True)
    e2 = jnp.exp(v2 - v1)
    g1 = 1.0 / (1.0 + e2)
    g2 = e2 / (1.0 + e2)
    out = jnp.where(lane == 0, i1.astype(F32), 0.0)
    out = jnp.where(lane == 1, i2.astype(F32), out)
    out = jnp.where(lane == 2, g1, out)
    out = jnp.where(lane == 3, g2, out)
    out_ref[...] = out


def _router(h, g, router_w, router_b):
    n, d = h.shape
    nexp = router_w.shape[1]
    tm = min(NORM_TM, n)
    w = jnp.zeros((d, LANES), F32).at[:, :nexp].set(router_w)
    wh = w.astype(BF16)
    wl = (w - wh.astype(F32)).astype(BF16)
    rb = jnp.zeros((1, LANES), F32).at[0, :nexp].set(router_b)
    return pl.pallas_call(
        functools.partial(_router_kernel, nexp=nexp),
        grid=(n // tm,),
        in_specs=[
            pl.BlockSpec((tm, d), lambda i: (i, 0)),
            pl.BlockSpec((1, d), lambda i: (0, 0)),
            pl.BlockSpec((d, LANES), lambda i: (0, 0)),
            pl.BlockSpec((d, LANES), lambda i: (0, 0)),
            pl.BlockSpec((1, LANES), lambda i: (0, 0)),
        ],
        out_specs=[
            pl.BlockSpec((tm, LANES), lambda i: (i, 0)),
            pl.BlockSpec((tm, d // 2), lambda i: (i, 0)),
        ],
        out_shape=[
            jax.ShapeDtypeStruct((n, LANES), F32),
            jax.ShapeDtypeStruct((n, d // 2), I32),
        ],
        compiler_params=_cparams(1),
        name="router",
    )(h, g.reshape(1, d), wh, wl, rb)


def _gffn_kernel(te_ref, nv_ref, cnt_ref, src_ref, xp_hbm, wg_ref, wu_ref, wd_ref, out_ref,
                 xg_ref, xn_ref, sem, *, tm, sub, issue_steps):
    t = pl.program_id(0)
    f = pl.program_id(1)
    nf = pl.num_programs(1)
    nv = nv_ref[0]
    cnt = cnt_ref[t]
    half = xg_ref.shape[2]
    rows_per_step = tm // issue_steps
    weights = (wg_ref, wu_ref, wd_ref)

    def slot_copy(slot):
        return pltpu.make_async_copy(xp_hbm.at[pl.ds(0, tm)], xg_ref.at[slot], sem.at[slot])

    def row_copy(tile, slot, r):
        tok = src_ref[tile * tm + r]
        return pltpu.make_async_copy(xp_hbm.at[pl.ds(tok, 1)],
                                     xg_ref.at[slot, pl.ds(r, 1)], sem.at[slot])

    def prefetch_rows():
        nxt = jnp.minimum(t + 1, nv - 1)
        slot = (t + 1) % 2
        r0 = f * rows_per_step
        for r in range(rows_per_step):
            row_copy(nxt, slot, r0 + r).start()

    @pl.when(t < nv)
    def _():
        @pl.when(f == 0)
        def _():
            @pl.when(t == 0)
            def _():
                def body(r, carry):
                    row_copy(0, 0, r).start()
                    return carry

                lax.fori_loop(0, tm, body, 0, unroll=DMA_UNROLL)

            slot_copy(t % 2).wait()
            p = xg_ref[t % 2]
            xn_ref[:, :half] = lax.bitcast_convert_type(lax.shift_left(p, 16), F32).astype(BF16)
            xn_ref[:, half:] = lax.bitcast_convert_type(p & HI16, F32).astype(BF16)
            out_ref[...] = jnp.zeros_like(out_ref)

        full = cnt == tm
        issuing = f < issue_steps

        @pl.when(jnp.logical_and(full, issuing))
        def _():
            prefetch_rows()
            _swiglu_step(xn_ref, *weights, out_ref)

        @pl.when(jnp.logical_and(full, jnp.logical_not(issuing)))
        def _():
            _swiglu_step(xn_ref, *weights, out_ref)

        @pl.when(jnp.logical_not(full))
        def _():
            @pl.when(issuing)
            def _():
                prefetch_rows()

            for q in range(tm // sub):
                @pl.when(q * sub < cnt)
                def _():
                    _swiglu_step(xn_ref, *weights, out_ref, rows=slice(q * sub, (q + 1) * sub))

        @pl.when(jnp.logical_and(t == nv - 1, f == nf - 1))
        def _():
            slot_copy((t + 1) % 2).wait()

    @pl.when(jnp.logical_and(t >= nv, f == 0))
    def _():
        out_ref[...] = jnp.zeros_like(out_ref)


def _grouped_ffn(xp, wg, wu, wd, j, tile_expert, n_valid, tile_rows, src_token, *, tm):
    half = xp.shape[1]
    d = 2 * half
    slots = src_token.shape[0]
    ff = wg.shape[3]
    tf = min(FFN_TF, ff)
    nf = ff // tf
    ntiles = slots // tm
    sub = min(MOE_SUB, tm)
    issue_steps = 1
    while issue_steps * 2 <= nf and tm % (issue_steps * 2) == 0:
        issue_steps *= 2

    def fcol(t, f, nv):
        return jnp.where(t < nv[0], f, nf - 1)

    grid_spec = pltpu.PrefetchScalarGridSpec(
        num_scalar_prefetch=4,
        grid=(ntiles, nf),
        in_specs=[
            pl.BlockSpec(memory_space=pl.ANY),
            pl.BlockSpec((None, None, d, tf),
                         lambda t, f, te, nv, cnt, src: (j, te[t], 0, fcol(t, f, nv))),
            pl.BlockSpec((None, None, d, tf),
                         lambda t, f, te, nv, cnt, src: (j, te[t], 0, fcol(t, f, nv))),
            pl.BlockSpec((None, None, tf, d),
                         lambda t, f, te, nv, cnt, src: (j, te[t], fcol(t, f, nv), 0)),
        ],
        out_specs=pl.BlockSpec((tm, d), lambda t, f, te, nv, cnt, src: (t, 0)),
        scratch_shapes=[
            pltpu.VMEM((2, tm, half), I32),
            pltpu.VMEM((tm, d), BF16),
            pltpu.SemaphoreType.DMA((2,)),
        ],
    )
    return pl.pallas_call(
        functools.partial(_gffn_kernel, tm=tm, sub=sub, issue_steps=issue_steps),
        grid_spec=grid_spec,
        out_shape=jax.ShapeDtypeStruct((slots, d), F32),
        compiler_params=_cparams(2),
        name="grouped_ffn",
    )(tile_expert, n_valid, tile_rows, src_token, xp, wg, wu, wd)


def _combine_kernel(slot_ref, h_ref, rt_ref, g_ref, ys_hbm, out_ref, buf_ref, sem,
                    *, tm, ntok, final_norm):
    i = pl.program_id(0)
    nt = pl.num_programs(0)

    def row_copy(tile, slot, kk, r):
        sl = slot_ref[kk * ntok + tile * tm + r]
        return pltpu.make_async_copy(ys_hbm.at[pl.ds(sl, 1)],
                                     buf_ref.at[slot, kk, pl.ds(r, 1)], sem.at[slot])

    def issue(tile, slot):
        def body(r, carry):
            for kk in range(TOP_K):
                row_copy(tile, slot, kk, r).start()
            return carry

        lax.fori_loop(0, tm, body, 0, unroll=DMA_UNROLL)

    @pl.when(i == 0)
    def _():
        issue(0, 0)

    @pl.when(i + 1 < nt)
    def _():
        issue(i + 1, (i + 1) % 2)

    slot = i % 2
    for kk in range(TOP_K):
        pltpu.make_async_copy(ys_hbm.at[pl.ds(0, tm)], buf_ref.at[slot, kk], sem.at[slot]).wait()
    rt = rt_ref[...]
    out = h_ref[...] + rt[:, 2:3] * buf_ref[slot, 0] + rt[:, 3:4] * buf_ref[slot, 1]
    if final_norm:
        out = _rms(out, g_ref[...])
    out_ref[...] = out


def _combine(h, routing, ys, slot_of, final_g, *, tm):
    n, d = h.shape
    g = jnp.ones((1, d), F32) if final_g is None else final_g.reshape(1, d)
    grid_spec = pltpu.PrefetchScalarGridSpec(
        num_scalar_prefetch=1,
        grid=(n // tm,),
        in_specs=[
            pl.BlockSpec((tm, d), lambda i, sl: (i, 0)),
            pl.BlockSpec((tm, LANES), lambda i, sl: (i, 0)),
            pl.BlockSpec((1, d), lambda i, sl: (0, 0)),
            pl.BlockSpec(memory_space=pl.ANY),
        ],
        out_specs=pl.BlockSpec((tm, d), lambda i, sl: (i, 0)),
        scratch_shapes=[pltpu.VMEM((2, TOP_K, tm, d), F32), pltpu.SemaphoreType.DMA((2,))],
    )
    return pl.pallas_call(
        functools.partial(_combine_kernel, tm=tm, ntok=n, final_norm=final_g is not None),
        grid_spec=grid_spec,
        out_shape=jax.ShapeDtypeStruct((n, d), F32),
        compiler_params=_cparams(1),
        name="combine",
    )(slot_of, h, routing, g, ys)


def _moe(h, g, router_w, router_b, wg, wu, wd, j, final_g):
    n, d = h.shape
    nexp = router_w.shape[1]
    tm = min(MOE_TM, n)
    routing, xp = _router(h, g, router_w, router_b)
    experts = routing[:, :TOP_K].astype(I32).T.reshape(-1)
    onehot = (experts[:, None] == jnp.arange(nexp, dtype=I32)[None, :]).astype(I32)
    rank = jnp.sum((jnp.cumsum(onehot, axis=0) - onehot) * onehot, axis=1)
    counts = jnp.sum(onehot, axis=0)
    tiles_per = (counts + tm - 1) // tm
    tile_end = jnp.cumsum(tiles_per)
    tile_start = tile_end - tiles_per
    slot_of = (tile_start[experts] * tm + rank).astype(I32)
    max_tiles = (TOP_K * n) // tm + nexp
    tokens = jnp.tile(jnp.arange(n, dtype=I32), TOP_K)
    src_token = jnp.zeros((max_tiles * tm,), I32).at[slot_of].set(tokens)
    n_valid = tile_end[-1:].astype(I32)
    tile_ids = jnp.arange(max_tiles, dtype=I32)
    tile_expert = jnp.minimum(
        jnp.sum((tile_ids[:, None] >= tile_end[None, :]).astype(I32), axis=1), nexp - 1)
    tile_rows = jnp.clip(counts[tile_expert] - (tile_ids - tile_start[tile_expert]) * tm,
                         0, tm).astype(I32)
    ys = _grouped_ffn(xp, wg, wu, wd, j, tile_expert, n_valid, tile_rows, src_token, tm=tm)
    return _combine(h, routing, ys, slot_of, final_g, tm=min(COMBINE_TM, n))


def _norm_kernel(x_ref, g_ref, out_ref):
    out_ref[...] = _rms(x_ref[...], g_ref[...])


def _final_norm(h, g):
    n, d = h.shape
    tm = min(NORM_TM, n)
    return pl.pallas_call(
        _norm_kernel,
        grid=(n // tm,),
        in_specs=[pl.BlockSpec((tm, d), lambda i: (i, 0)),
                  pl.BlockSpec((1, d), lambda i: (0, 0))],
        out_specs=pl.BlockSpec((tm, d), lambda i: (i, 0)),
        out_shape=jax.ShapeDtypeStruct((n, d), F32),
        compiler_params=_cparams(1),
        name="final_norm",
    )(h, g.reshape(1, d))


def kernel(x, norm1_g, norm2_g, final_g, w_in, mlstm_gate_b, conv_w, mlstm_norm_g, pool_w, pool_scale, w_branch_m, w_branch_p, w_out, ffn_w_gate, ffn_w_up, ffn_w_down, router_w, router_b, exp_w_gate, exp_w_up, exp_w_down):
    batch, seq, d = x.shape
    depth = w_in.shape[0]
    nheads = mlstm_gate_b.shape[1] // 2
    dm = conv_w.shape[2] // 2
    dh = dm // nheads
    dp = pool_scale.shape[1]
    col_o_end = 4 * dm
    col_p = col_o_end + 2 * nheads
    col_g = col_p + dp
    assert col_o_end % d == 0 and (col_o_end + 2 * d) % dp == 0
    w_main = jnp.concatenate(
        [w_in[:, :, :col_o_end], w_in[:, :, col_g:], w_in[:, :, col_p:col_g]], axis=-1).astype(BF16)
    w_if = jnp.zeros((depth, d, LANES), F32).at[:, :, :2 * nheads].set(
        w_in[:, :, col_o_end:col_p]).astype(BF16)
    gate_block = col_o_end // d
    pool_block = (col_o_end + 2 * d) // dp
    wbm = w_branch_m.astype(BF16)
    wbp = w_branch_p.astype(BF16)
    wo = w_out.astype(BF16)
    pw = pool_w.astype(BF16)

    h = x.reshape(batch * seq, d)
    for l in range(depth):
        z, gates = _inproj(h, norm1_g[l], w_main, w_if, conv_w[l], l, seq=seq, dm=dm, dh=dh)
        hm = _mlstm(z, gates, mlstm_gate_b[l], mlstm_norm_g[l],
                    batch=batch, seq=seq, nheads=nheads, dh=dh)
        hp = _pool(z, pw[l], pool_scale[l], seq=seq, col_block=pool_block)
        h = _merge(h, hm, hp, z, wbm, wbp, wo, l, gate_block=gate_block)
        j = l // 2
        if l % 2 == 0:
            h = _ffn(h, norm2_g[l], ffn_w_gate, ffn_w_up, ffn_w_down, j)
        else:
            fg = final_g if l == depth - 1 else None
            h = _moe(h, norm2_g[l], router_w[j], router_b[j],
                     exp_w_gate, exp_w_up, exp_w_down, j, fg)
    if depth % 2 == 1:
        h = _final_norm(h, final_g)
    return h.reshape(batch, seq, d)
```

```python
import functools

import jax
import jax.numpy as jnp
from jax import lax
from jax.experimental import pallas as pl
from jax.experimental.pallas import tpu as pltpu

F32 = jnp.float32
BF16 = jnp.bfloat16
I32 = jnp.int32

EPS = 1e-6
CONV_K = 4
POOL_WINDOWS = (2, 4, 8, 16)
POOL_HALO = 16
TOP_K = 2
LANES = 128
SUBLANES = 8
NEG_BIG = -1e30
HI16 = -65536
VMEM_LIMIT = 56 * 1024 * 1024

INPROJ_TM = 1024
INPROJ_TN = 1024
INPROJ_SUB = 256
MLSTM_T = 512
MLSTM_CHUNK = 256
POOL_T = 512
MERGE_TM = 512
FFN_TM = 1024
FFN_TF = 512
FFN_SUB = 256
MOE_TM = 1024
MOE_SUB = 256
COMBINE_TM = 256
NORM_TM = 1024
DMA_UNROLL = 8


def _cparams(n_axes):
    return pltpu.CompilerParams(
        dimension_semantics=("arbitrary",) * n_axes,
        vmem_limit_bytes=VMEM_LIMIT,
    )


def _rms(x, g):
    ms = jnp.mean(x * x, axis=-1, keepdims=True)
    return x * lax.rsqrt(ms + EPS) * g


def _sigmoid(x):
    return 1.0 / (1.0 + jnp.exp(-x))


def _inproj_kernel(x_ref, g_ref, w_ref, wif_ref, cw_ref, z_ref, gates_ref, xn_ref, halo_ref,
                   *, tiles_per_seq, n_q, n_qk, n_qkvo, k_scale):
    i = pl.program_id(0)
    j = pl.program_id(1)
    tm = x_ref.shape[0]
    n_v_end = n_qk + (n_qk - n_q)

    @pl.when(jnp.logical_and(i == 0, j == 0))
    def _():
        halo_ref[...] = jnp.zeros_like(halo_ref)

    @pl.when(j == 0)
    def _():
        xn = _rms(x_ref[...], g_ref[...]).astype(BF16)
        xn_ref[...] = xn
        gates_ref[...] = jnp.dot(xn, wif_ref[...], preferred_element_type=F32)

    tn = z_ref.shape[1]
    sub = min(INPROJ_SUB, tn)
    col_blocks = [slice(c, c + sub) for c in range(0, tn, sub)]

    def project(cs):
        return jnp.dot(xn_ref[...], w_ref[:, cs], preferred_element_type=F32)

    @pl.when(j < n_qk)
    def _():
        first = i % tiles_per_seq == 0
        scale = jnp.where(j >= n_q, k_scale, 1.0)
        for cs in col_blocks:
            acc = project(cs)
            halo = jnp.where(first, 0.0, halo_ref[j, :, cs])
            ext = jnp.concatenate([halo, acc], axis=0)
            w = cw_ref[:, cs]
            y = w[CONV_K - 1:CONV_K] * acc
            for d in range(1, CONV_K):
                y = y + w[CONV_K - 1 - d:CONV_K - d] * pltpu.roll(ext, d, axis=0)[SUBLANES:]
            halo_ref[j, :, cs] = acc[tm - SUBLANES:]
            y = y * _sigmoid(y) * scale
            z_ref[:, cs] = y.astype(BF16)

    @pl.when(jnp.logical_and(j >= n_v_end, j < n_qkvo))
    def _():
        for cs in col_blocks:
            z_ref[:, cs] = _sigmoid(project(cs)).astype(BF16)

    @pl.when(jnp.logical_or(jnp.logical_and(j >= n_qk, j < n_v_end), j >= n_qkvo))
    def _():
        z_ref[...] = project(slice(None)).astype(BF16)


def _inproj(h, g, w_main, w_if, conv_w, layer, *, seq, dm, dh):
    n, d = h.shape
    cols = w_main.shape[2]
    tm = min(INPROJ_TM, n, seq)
    tn = min(INPROJ_TN, dm)
    assert seq % tm == 0 and dm % tn == 0 and cols % tn == 0
    n_q = dm // tn
    cw = jnp.zeros((SUBLANES, 2 * dm), F32).at[:CONV_K].set(conv_w)
    return pl.pallas_call(
        functools.partial(_inproj_kernel, tiles_per_seq=seq // tm, n_q=n_q, n_qk=2 * n_q,
                          n_qkvo=4 * n_q, k_scale=dh ** -0.5),
        grid=(n // tm, cols // tn),
        in_specs=[
            pl.BlockSpec((tm, d), lambda i, j: (i, 0)),
            pl.BlockSpec((1, d), lambda i, j: (0, 0)),
            pl.BlockSpec((None, d, tn), lambda i, j: (layer, 0, j)),
            pl.BlockSpec((None, d, LANES), lambda i, j: (layer, 0, 0)),
            pl.BlockSpec((SUBLANES, tn), lambda i, j: (0, jnp.minimum(j, 2 * n_q - 1))),
        ],
        out_specs=[
            pl.BlockSpec((tm, tn), lambda i, j: (i, j)),
            pl.BlockSpec((tm, LANES), lambda i, j: (i, 0)),
        ],
        out_shape=[
            jax.ShapeDtypeStruct((n, cols), BF16),
            jax.ShapeDtypeStruct((n, LANES), F32),
        ],
        scratch_shapes=[pltpu.VMEM((tm, d), BF16),
                        pltpu.VMEM((2 * n_q, SUBLANES, tn), F32)],
        compiler_params=_cparams(2),
        name="inproj",
    )(h, g.reshape(1, d), w_main, w_if, cw)


def _mlstm_kernel(q_ref, k_ref, v_ref, og_ref, gt_ref, gb_ref, ng_ref,
                  out_ref, ct_ref, n_ref, m_ref, *, chunk, nheads):
    t_rows, dm = q_ref.shape
    dh = dm // nheads

    @pl.when(pl.program_id(1) == 0)
    def _():
        ct_ref[...] = jnp.zeros_like(ct_ref)
        n_ref[...] = jnp.zeros_like(n_ref)
        m_ref[...] = jnp.zeros_like(m_ref)

    gts = gt_ref[...] + gb_ref[...]
    lf = jnp.minimum(gts, 0.0) - jnp.log(1.0 + jnp.exp(-jnp.abs(gts)))
    in_chunk = lax.broadcasted_iota(I32, gts.shape, 0) & (chunk - 1)
    b = lf
    d = 1
    while d < chunk:
        b = b + jnp.where(in_chunk >= d, pltpu.roll(b, d, axis=0), 0.0)
        d *= 2
    gts_t = gts.T
    b_t = b.T

    tt = lax.broadcasted_iota(I32, (chunk, chunk), 0)
    ss = lax.broadcasted_iota(I32, (chunk, chunk), 1)
    causal = tt >= ss

    for hd in range(nheads):
        cs = slice(hd * dh, (hd + 1) * dh)
        icol = gts[:, hd:hd + 1]
        bcol = b[:, nheads + hd:nheads + hd + 1]
        irow = gts_t[hd:hd + 1, :]
        brow = b_t[nheads + hd:nheads + hd + 1, :]
        ct = ct_ref[hd]
        nvec = n_ref[hd]
        m_prev = m_ref[:, hd:hd + 1]
        ng = ng_ref[:, cs]
        for c in range(t_rows // chunk):
            lo, hi = c * chunk, (c + 1) * chunk
            qb, kb = q_ref[lo:hi, cs], k_ref[lo:hi, cs]
            qc, kc = qb.astype(F32), kb.astype(F32)
            vc = v_ref[lo:hi, cs]
            bc, ic = bcol[lo:hi], icol[lo:hi]
            br, ir = brow[:, lo:hi], irow[:, lo:hi]
            dmat = jnp.where(causal, bc - br + ir, NEG_BIG)
            inter = bc + m_prev
            mt = jnp.maximum(inter, jnp.max(dmat, axis=1, keepdims=True))
            wi = jnp.exp(inter - mt)
            s = lax.dot_general(qb, kb, (((1,), (1,)), ((), ())),
                                preferred_element_type=F32) * jnp.exp(dmat - mt)
            num = wi * jnp.dot(qb, ct.astype(BF16), preferred_element_type=F32)
            num = num + jnp.dot(s.astype(BF16), vc, preferred_element_type=F32)
            den = wi * jnp.sum(qc * nvec, axis=1, keepdims=True)
            den = den + jnp.sum(s, axis=1, keepdims=True)
            hr = num / jnp.maximum(jnp.abs(den), jnp.exp(-mt))
            hn = _rms(hr, ng)
            out_ref[lo:hi, cs] = (og_ref[lo:hi, cs].astype(F32) * hn).astype(BF16)
            bl = bc[chunk - 1:chunk]
            gcol = bl - bc + ic
            m_new = jnp.maximum(bl + m_prev, jnp.max(gcol, axis=0, keepdims=True))
            a = jnp.exp(bl + m_prev - m_new)
            kw = kc * jnp.exp(gcol - m_new)
            ct = a * ct + lax.dot_general(kw.astype(BF16), vc, (((0,), (0,)), ((), ())),
                                          preferred_element_type=F32)
            nvec = a * nvec + jnp.sum(kw, axis=0, keepdims=True)
            m_prev = m_new
        ct_ref[hd] = ct
        n_ref[hd] = nvec
        m_ref[:, hd:hd + 1] = m_prev


def _mlstm(z, gates, gate_b, norm_g, *, batch, seq, nheads, dh):
    n = z.shape[0]
    t = min(MLSTM_T, seq)
    chunk = min(MLSTM_CHUNK, t)
    spb = seq // t
    dm = nheads * dh
    gb = jnp.zeros((1, LANES), F32).at[0, :2 * nheads].set(gate_b)

    def zspec(col_block):
        return pl.BlockSpec((t, dm), lambda b, s: (b * spb + s, col_block))

    return pl.pallas_call(
        functools.partial(_mlstm_kernel, chunk=chunk, nheads=nheads),
        grid=(batch, spb),
        in_specs=[
            zspec(0), zspec(1), zspec(2), zspec(3),
            pl.BlockSpec((t, LANES), lambda b, s: (b * spb + s, 0)),
            pl.BlockSpec((1, LANES), lambda b, s: (0, 0)),
            pl.BlockSpec((1, dm), lambda b, s: (0, 0)),
        ],
        out_specs=pl.BlockSpec((t, dm), lambda b, s: (b * spb + s, 0)),
        out_shape=jax.ShapeDtypeStruct((n, dm), BF16),
        scratch_shapes=[
            pltpu.VMEM((nheads, dh, dh), F32),
            pltpu.VMEM((nheads, 1, dh), F32),
            pltpu.VMEM((1, LANES), F32),
        ],
        compiler_params=_cparams(2),
        name="mlstm",
    )(z, z, z, z, gates, gb, norm_g.reshape(1, dm))


def _pool_kernel(p_ref, halo_ref, w_ref, sc_ref, out_ref, *, tiles_per_seq):
    t_rows = p_ref.shape[0]
    grp = w_ref.shape[1]
    pos0 = (pl.program_id(0) % tiles_per_seq) * t_rows
    p = p_ref[...].astype(F32)
    halo = jnp.where(pos0 == 0, 0.0, halo_ref[...].astype(F32))
    ext = jnp.concatenate([halo, p], axis=0)
    pos = pos0 + lax.broadcasted_iota(I32, (t_rows, 1), 0)
    posf = (pos + 1).astype(F32)
    for j, w in enumerate(POOL_WINDOWS):
        cs = slice(j * grp, (j + 1) * grp)
        win = ext[:, cs]
        d = 1
        while d < w:
            win = win + pltpu.roll(win, d, axis=0)
            d *= 2
        pooled = win[POOL_HALO:] / jnp.minimum(posf, float(w)) - p[:, cs]
        y = jnp.dot(pooled.astype(BF16), w_ref[j], preferred_element_type=F32)
        out_ref[:, cs] = (y * sc_ref[:, cs]).astype(BF16)


def _pool(z, pool_w, pool_scale, *, seq, col_block):
    n = z.shape[0]
    ngrp, grp, _ = pool_w.shape
    dp = ngrp * grp
    t = min(POOL_T, seq)
    tps = seq // t
    hb = t // POOL_HALO
    return pl.pallas_call(
        functools.partial(_pool_kernel, tiles_per_seq=tps),
        grid=(n // t,),
        in_specs=[
            pl.BlockSpec((t, dp), lambda i: (i, col_block)),
            pl.BlockSpec((POOL_HALO, dp), lambda i: (jnp.maximum(i * hb - 1, 0), col_block)),
            pl.BlockSpec((ngrp, grp, grp), lambda i: (0, 0, 0)),
            pl.BlockSpec((1, dp), lambda i: (0, 0)),
        ],
        out_specs=pl.BlockSpec((t, dp), lambda i: (i, 0)),
        out_shape=jax.ShapeDtypeStruct((n, dp), BF16),
        compiler_params=_cparams(1),
        name="pool",
    )(z, z, pool_w, pool_scale.reshape(1, dp))


def _merge_kernel(h_ref, hm_ref, hp_ref, gm_ref, gp_ref, wbm_ref, wbp_ref, wo_ref, out_ref):
    ym = jnp.dot(hm_ref[...], wbm_ref[...], preferred_element_type=F32)
    yp = jnp.dot(hp_ref[...], wbp_ref[...], preferred_element_type=F32)
    y = _sigmoid(gm_ref[...].astype(F32)) * ym + _sigmoid(gp_ref[...].astype(F32)) * yp
    out_ref[...] = h_ref[...] + jnp.dot(y.astype(BF16), wo_ref[...],
                                        preferred_element_type=F32)


def _merge(h, hm, hp, z, wbm, wbp, wo, layer, *, gate_block):
    n, d = h.shape
    dm, dp = hm.shape[1], hp.shape[1]
    tm = min(MERGE_TM, n)
    once = pl.Buffered(1)
    return pl.pallas_call(
        _merge_kernel,
        grid=(n // tm,),
        in_specs=[
            pl.BlockSpec((tm, d), lambda i: (i, 0)),
            pl.BlockSpec((tm, dm), lambda i: (i, 0)),
            pl.BlockSpec((tm, dp), lambda i: (i, 0)),
            pl.BlockSpec((tm, d), lambda i: (i, gate_block)),
            pl.BlockSpec((tm, d), lambda i: (i, gate_block + 1)),
            pl.BlockSpec((None, dm, d), lambda i: (layer, 0, 0), pipeline_mode=once),
            pl.BlockSpec((None, dp, d), lambda i: (layer, 0, 0), pipeline_mode=once),
            pl.BlockSpec((None, d, d), lambda i: (layer, 0, 0), pipeline_mode=once),
        ],
        out_specs=pl.BlockSpec((tm, d), lambda i: (i, 0)),
        out_shape=jax.ShapeDtypeStruct((n, d), F32),
        compiler_params=_cparams(1),
        name="merge",
    )(h, hm, hp, z, z, wbm, wbp, wo)


def _swiglu_step(xn_ref, wg_ref, wu_ref, wd_ref, out_ref, rows=slice(None)):
    xn = xn_ref[rows, :]
    tf = wg_ref.shape[1]
    sub = min(FFN_SUB, tf)
    for c in range(0, tf, sub):
        cs = slice(c, c + sub)
        a = jnp.dot(xn, wg_ref[:, cs].astype(BF16), preferred_element_type=F32)
        u = jnp.dot(xn, wu_ref[:, cs].astype(BF16), preferred_element_type=F32)
        act = (a * _sigmoid(a) * u).astype(BF16)
        out_ref[rows, :] += jnp.dot(act, wd_ref[cs, :].astype(BF16),
                                    preferred_element_type=F32)


def _ffn_kernel(x_hbm, g_ref, wg_ref, wu_ref, wd_ref, out_ref, xn_ref, sem):
    tm = out_ref.shape[0]

    @pl.when(pl.program_id(1) == 0)
    def _():
        rows = pl.ds(pl.multiple_of(pl.program_id(0) * tm, tm), tm)
        copy = pltpu.make_async_copy(x_hbm.at[rows], out_ref, sem)
        copy.start()
        copy.wait()
        xn_ref[...] = _rms(out_ref[...], g_ref[...]).astype(BF16)

    _swiglu_step(xn_ref, wg_ref, wu_ref, wd_ref, out_ref)


def _ffn(h, g, wg, wu, wd, j):
    n, d = h.shape
    ff = wg.shape[2]
    tm = min(FFN_TM, n)
    tf = min(FFN_TF, ff)
    return pl.pallas_call(
        _ffn_kernel,
        grid=(n // tm, ff // tf),
        in_specs=[
            pl.BlockSpec(memory_space=pl.ANY),
            pl.BlockSpec((1, d), lambda i, f: (0, 0)),
            pl.BlockSpec((None, d, tf), lambda i, f: (j, 0, f)),
            pl.BlockSpec((None, d, tf), lambda i, f: (j, 0, f)),
            pl.BlockSpec((None, tf, d), lambda i, f: (j, f, 0)),
        ],
        out_specs=pl.BlockSpec((tm, d), lambda i, f: (i, 0)),
        out_shape=jax.ShapeDtypeStruct((n, d), F32),
        scratch_shapes=[pltpu.VMEM((tm, d), BF16), pltpu.SemaphoreType.DMA(())],
        compiler_params=_cparams(2),
        name="ffn",
    )(h, g.reshape(1, d), wg, wu, wd)


def _router_kernel(x_ref, g_ref, wh_ref, wl_ref, rb_ref, out_ref, xp_ref, *, nexp):
    xn = _rms(x_ref[...], g_ref[...])
    xh = xn.astype(BF16)
    xhf = xh.astype(F32)
    xl = (xn - xhf).astype(BF16)
    half = xn.shape[1] // 2
    lo = lax.shift_right_logical(lax.bitcast_convert_type(xhf[:, :half], I32), 16)
    hi = lax.bitcast_convert_type(xhf[:, half:], I32) & HI16
    xp_ref[...] = hi | lo
    wh, wl = wh_ref[...], wl_ref[...]
    logits = (jnp.dot(xh, wh, preferred_element_type=F32)
              + jnp.dot(xh, wl, preferred_element_type=F32)
              + jnp.dot(xl, wh, preferred_element_type=F32)) + rb_ref[...]
    lane = lax.broadcasted_iota(I32, logits.shape, 1)
    logits = jnp.where(lane < nexp, logits, NEG_BIG)
    v1 = jnp.max(logits, axis=1, keepdims=True)
    i1 = jnp.min(jnp.where(logits == v1, lane, LANES), axis=1, keepdims=True)
    rest = jnp.where(lane == i1, NEG_BIG, logits)
    v2 = jnp.max(rest, axis=1, keepdims=True)
    i2 = jnp.min(jnp.where(rest == v2, lane, LANES), axis=1, keepdims=True)
    e2 = jnp.exp(v2 - v1)
    g1 = 1.0 / (1.0 + e2)
    g2 = e2 / (1.0 + e2)
    out = jnp.where(lane == 0, i1.astype(F32), 0.0)
    out = jnp.where(lane == 1, i2.astype(F32), out)
    out = jnp.where(lane == 2, g1, out)
    out = jnp.where(lane == 3, g2, out)
    out_ref[...] = out


def _router(h, g, router_w, router_b):
    n, d = h.shape
    nexp = router_w.shape[1]
    tm = min(NORM_TM, n)
    w = jnp.zeros((d, LANES), F32).at[:, :nexp].set(router_w)
    wh = w.astype(BF16)
    wl = (w - wh.astype(F32)).astype(BF16)
    rb = jnp.zeros((1, LANES), F32).at[0, :nexp].set(router_b)
    return pl.pallas_call(
        functools.partial(_router_kernel, nexp=nexp),
        grid=(n // tm,),
        in_specs=[
            pl.BlockSpec((tm, d), lambda i: (i, 0)),
            pl.BlockSpec((1, d), lambda i: (0, 0)),
            pl.BlockSpec((d, LANES), lambda i: (0, 0)),
            pl.BlockSpec((d, LANES), lambda i: (0, 0)),
            pl.BlockSpec((1, LANES), lambda i: (0, 0)),
        ],
        out_specs=[
            pl.BlockSpec((tm, LANES), lambda i: (i, 0)),
            pl.BlockSpec((tm, d // 2), lambda i: (i, 0)),
        ],
        out_shape=[
            jax.ShapeDtypeStruct((n, LANES), F32),
            jax.ShapeDtypeStruct((n, d // 2), I32),
        ],
        compiler_params=_cparams(1),
        name="router",
    )(h, g.reshape(1, d), wh, wl, rb)


def _gffn_kernel(te_ref, nv_ref, cnt_ref, src_ref, xp_hbm, wg_ref, wu_ref, wd_ref, out_ref,
                 xg_ref, xn_ref, sem, *, tm, sub, issue_steps):
    t = pl.program_id(0)
    f = pl.program_id(1)
    nf = pl.num_programs(1)
    nv = nv_ref[0]
    cnt = cnt_ref[t]
    half = xg_ref.shape[2]
    rows_per_step = tm // issue_steps
    weights = (wg_ref, wu_ref, wd_ref)

    def slot_copy(slot):
        return pltpu.make_async_copy(xp_hbm.at[pl.ds(0, tm)], xg_ref.at[slot], sem.at[slot])

    def row_copy(tile, slot, r):
        tok = src_ref[tile * tm + r]
        return pltpu.make_async_copy(xp_hbm.at[pl.ds(tok, 1)],
                                     xg_ref.at[slot, pl.ds(r, 1)], sem.at[slot])

    def prefetch_rows():
        nxt = jnp.minimum(t + 1, nv - 1)
        slot = (t + 1) % 2
        r0 = f * rows_per_step
        for r in range(rows_per_step):
            row_copy(nxt, slot, r0 + r).start()

    @pl.when(t < nv)
    def _():
        @pl.when(f == 0)
        def _():
            @pl.when(t == 0)
            def _():
                def body(r, carry):
                    row_copy(0, 0, r).start()
                    return carry

                lax.fori_loop(0, tm, body, 0, unroll=DMA_UNROLL)

            slot_copy(t % 2).wait()
            p = xg_ref[t % 2]
            xn_ref[:, :half] = lax.bitcast_convert_type(lax.shift_left(p, 16), F32).astype(BF16)
            xn_ref[:, half:] = lax.bitcast_convert_type(p & HI16, F32).astype(BF16)
            out_ref[...] = jnp.zeros_like(out_ref)

        full = cnt == tm
        issuing = f < issue_steps

        @pl.when(jnp.logical_and(full, issuing))
        def _():
            prefetch_rows()
            _swiglu_step(xn_ref, *weights, out_ref)

        @pl.when(jnp.logical_and(full, jnp.logical_not(issuing)))
        def _():
            _swiglu_step(xn_ref, *weights, out_ref)

        @pl.when(jnp.logical_not(full))
        def _():
            @pl.when(issuing)
            def _():
                prefetch_rows()

            for q in range(tm // sub):
                @pl.when(q * sub < cnt)
                def _():
                    _swiglu_step(xn_ref, *weights, out_ref, rows=slice(q * sub, (q + 1) * sub))

        @pl.when(jnp.logical_and(t == nv - 1, f == nf - 1))
        def _():
            slot_copy((t + 1) % 2).wait()

    @pl.when(jnp.logical_and(t >= nv, f == 0))
    def _():
        out_ref[...] = jnp.zeros_like(out_ref)


def _grouped_ffn(xp, wg, wu, wd, j, tile_expert, n_valid, tile_rows, src_token, *, tm):
    half = xp.shape[1]
    d = 2 * half
    slots = src_token.shape[0]
    ff = wg.shape[3]
    tf = min(FFN_TF, ff)
    nf = ff // tf
    ntiles = slots // tm
    sub = min(MOE_SUB, tm)
    issue_steps = 1
    while issue_steps * 2 <= nf and tm % (issue_steps * 2) == 0:
        issue_steps *= 2

    def fcol(t, f, nv):
        return jnp.where(t < nv[0], f, nf - 1)

    grid_spec = pltpu.PrefetchScalarGridSpec(
        num_scalar_prefetch=4,
        grid=(ntiles, nf),
        in_specs=[
            pl.BlockSpec(memory_space=pl.ANY),
            pl.BlockSpec((None, None, d, tf),
                         lambda t, f, te, nv, cnt, src: (j, te[t], 0, fcol(t, f, nv))),
            pl.BlockSpec((None, None, d, tf),
                         lambda t, f, te, nv, cnt, src: (j, te[t], 0, fcol(t, f, nv))),
            pl.BlockSpec((None, None, tf, d),
                         lambda t, f, te, nv, cnt, src: (j, te[t], fcol(t, f, nv), 0)),
        ],
        out_specs=pl.BlockSpec((tm, d), lambda t, f, te, nv, cnt, src: (t, 0),
                               pipeline_mode=pl.Buffered(1)),
        scratch_shapes=[
            pltpu.VMEM((2, tm, half), I32),
            pltpu.VMEM((tm, d), BF16),
            pltpu.SemaphoreType.DMA((2,)),
        ],
    )
    return pl.pallas_call(
        functools.partial(_gffn_kernel, tm=tm, sub=sub, issue_steps=issue_steps),
        grid_spec=grid_spec,
        out_shape=jax.ShapeDtypeStruct((slots, d), F32),
        compiler_params=_cparams(2),
        name="grouped_ffn",
    )(tile_expert, n_valid, tile_rows, src_token, xp, wg, wu, wd)


def _combine_kernel(slot_ref, h_ref, rt_ref, g_ref, ys_hbm, out_ref, buf_ref, sem,
                    *, tm, ntok, final_norm):
    i = pl.program_id(0)
    nt = pl.num_programs(0)

    def row_copy(tile, slot, kk, r):
        sl = slot_ref[kk * ntok + tile * tm + r]
        return pltpu.make_async_copy(ys_hbm.at[pl.ds(sl, 1)],
                                     buf_ref.at[slot, kk, pl.ds(r, 1)], sem.at[slot])

    def issue(tile, slot):
        def body(r, carry):
            for kk in range(TOP_K):
                row_copy(tile, slot, kk, r).start()
            return carry

        lax.fori_loop(0, tm, body, 0, unroll=DMA_UNROLL)

    @pl.when(i == 0)
    def _():
        issue(0, 0)

    @pl.when(i + 1 < nt)
    def _():
        issue(i + 1, (i + 1) % 2)

    slot = i % 2
    for kk in range(TOP_K):
        pltpu.make_async_copy(ys_hbm.at[pl.ds(0, tm)], buf_ref.at[slot, kk], sem.at[slot]).wait()
    rt = rt_ref[...]
    out = h_ref[...] + rt[:, 2:3] * buf_ref[slot, 0] + rt[:, 3:4] * buf_ref[slot, 1]
    if final_norm:
        out = _rms(out, g_ref[...])
    out_ref[...] = out


def _combine(h, routing, ys, slot_of, final_g, *, tm):
    n, d = h.shape
    g = jnp.ones((1, d), F32) if final_g is None else final_g.reshape(1, d)
    grid_spec = pltpu.PrefetchScalarGridSpec(
        num_scalar_prefetch=1,
        grid=(n // tm,),
        in_specs=[
            pl.BlockSpec((tm, d), lambda i, sl: (i, 0)),
            pl.BlockSpec((tm, LANES), lambda i, sl: (i, 0)),
            pl.BlockSpec((1, d), lambda i, sl: (0, 0)),
            pl.BlockSpec(memory_space=pl.ANY),
        ],
        out_specs=pl.BlockSpec((tm, d), lambda i, sl: (i, 0)),
        scratch_shapes=[pltpu.VMEM((2, TOP_K, tm, d), F32), pltpu.SemaphoreType.DMA((2,))],
    )
    return pl.pallas_call(
        functools.partial(_combine_kernel, tm=tm, ntok=n, final_norm=final_g is not None),
        grid_spec=grid_spec,
        out_shape=jax.ShapeDtypeStruct((n, d), F32),
        compiler_params=_cparams(1),
        name="combine",
    )(slot_of, h, routing, g, ys)


def _moe(h, g, router_w, router_b, wg, wu, wd, j, final_g):
    n, d = h.shape
    nexp = router_w.shape[1]
    tm = min(MOE_TM, n)
    routing, xp = _router(h, g, router_w, router_b)
    experts = routing[:, :TOP_K].astype(I32).T.reshape(-1)
    onehot = (experts[:, None] == jnp.arange(nexp, dtype=I32)[None, :]).astype(I32)
    rank = jnp.sum((jnp.cumsum(onehot, axis=0) - onehot) * onehot, axis=1)
    counts = jnp.sum(onehot, axis=0)
    tiles_per = (counts + tm - 1) // tm
    tile_end = jnp.cumsum(tiles_per)
    tile_start = tile_end - tiles_per
    slot_of = (tile_start[experts] * tm + rank).astype(I32)
    max_tiles = (TOP_K * n) // tm + nexp
    tokens = jnp.tile(jnp.arange(n, dtype=I32), TOP_K)
    src_token = jnp.zeros((max_tiles * tm,), I32).at[slot_of].set(tokens)
    n_valid = tile_end[-1:].astype(I32)
    tile_ids = jnp.arange(max_tiles, dtype=I32)
    tile_expert = jnp.minimum(
        jnp.sum((tile_ids[:, None] >= tile_end[None, :]).astype(I32), axis=1), nexp - 1)
    tile_rows = jnp.clip(counts[tile_expert] - (tile_ids - tile_start[tile_expert]) * tm,
                         0, tm).astype(I32)
    ys = _grouped_ffn(xp, wg, wu, wd, j, tile_expert, n_valid, tile_rows, src_token, tm=tm)
    return _combine(h, routing, ys, slot_of, final_g, tm=min(COMBINE_TM, n))


def _norm_kernel(x_ref, g_ref, out_ref):
    out_ref[...] = _rms(x_ref[...], g_ref[...])


def _final_norm(h, g):
    n, d = h.shape
    tm = min(NORM_TM, n)
    return pl.pallas_call(
        _norm_kernel,
        grid=(n // tm,),
        in_specs=[pl.BlockSpec((tm, d), lambda i: (i, 0)),
                  pl.BlockSpec((1, d), lambda i: (0, 0))],
        out_specs=pl.BlockSpec((tm, d), lambda i: (i, 0)),
        out_shape=jax.ShapeDtypeStruct((n, d), F32),
        compiler_params=_cparams(1),
        name="final_norm",
    )(h, g.reshape(1, d))


def kernel(x, norm1_g, norm2_g, final_g, w_in, mlstm_gate_b, conv_w, mlstm_norm_g, pool_w, pool_scale, w_branch_m, w_branch_p, w_out, ffn_w_gate, ffn_w_up, ffn_w_down, router_w, router_b, exp_w_gate, exp_w_up, exp_w_down):
    batch, seq, d = x.shape
    depth = w_in.shape[0]
    nheads = mlstm_gate_b.shape[1] // 2
    dm = conv_w.shape[2] // 2
    dh = dm // nheads
    dp = pool_scale.shape[1]
    col_o_end = 4 * dm
    col_p = col_o_end + 2 * nheads
    col_g = col_p + dp
    assert col_o_end % d == 0 and (col_o_end + 2 * d) % dp == 0
    w_main = jnp.concatenate(
        [w_in[:, :, :col_o_end], w_in[:, :, col_g:], w_in[:, :, col_p:col_g]], axis=-1).astype(BF16)
    w_if = jnp.zeros((depth, d, LANES), F32).at[:, :, :2 * nheads].set(
        w_in[:, :, col_o_end:col_p]).astype(BF16)
    gate_block = col_o_end // d
    pool_block = (col_o_end + 2 * d) // dp
    wbm = w_branch_m.astype(BF16)
    wbp = w_branch_p.astype(BF16)
    wo = w_out.astype(BF16)
    pw = pool_w.astype(BF16)

    h = x.reshape(batch * seq, d)
    for l in range(depth):
        z, gates = _inproj(h, norm1_g[l], w_main, w_if, conv_w[l], l, seq=seq, dm=dm, dh=dh)
        hm = _mlstm(z, gates, mlstm_gate_b[l], mlstm_norm_g[l],
                    batch=batch, seq=seq, nheads=nheads, dh=dh)
        hp = _pool(z, pw[l], pool_scale[l], seq=seq, col_block=pool_block)
        h = _merge(h, hm, hp, z, wbm, wbp, wo, l, gate_block=gate_block)
        j = l // 2
        if l % 2 == 0:
            h = _ffn(h, norm2_g[l], ffn_w_gate, ffn_w_up, ffn_w_down, j)
        else:
            fg = final_g if l == depth - 1 else None
            h = _moe(h, norm2_g[l], router_w[j], router_b[j],
                     exp_w_gate, exp_w_up, exp_w_down, j, fg)
    if depth % 2 == 1:
        h = _final_norm(h, final_g)
    return h.reshape(batch, seq, d)
```

```python
import functools

import jax
import jax.numpy as jnp
from jax import lax
from jax.experimental import pallas as pl
from jax.experimental.pallas import tpu as pltpu

F32 = jnp.float32
BF16 = jnp.bfloat16
I32 = jnp.int32

EPS = 1e-6
CONV_K = 4
POOL_WINDOWS = (2, 4, 8, 16)
POOL_HALO = 16
TOP_K = 2
LANES = 128
SUBLANES = 8
NEG_BIG = -1e30
HI16 = -65536
VMEM_LIMIT = 56 * 1024 * 1024
VMEM_LIMIT_MERGE = 59 * 1024 * 1024

INPROJ_TM = 1024
INPROJ_TN = 1024
INPROJ_SUB = 256
MLSTM_T = 512
MLSTM_CHUNK = 256
MERGE_TM = 512
MERGE_SUB = 512
FFN_TM = 1024
FFN_TF = 512
FFN_SUB = 256
MOE_TM = 1024
MOE_SUB = 256
COMBINE_TM = 256
NORM_TM = 1024
DMA_UNROLL = 8


def _cparams(n_axes, vmem_limit=VMEM_LIMIT):
    return pltpu.CompilerParams(
        dimension_semantics=("arbitrary",) * n_axes,
        vmem_limit_bytes=vmem_limit,
    )


def _rms(x, g):
    ms = jnp.mean(x * x, axis=-1, keepdims=True)
    return x * lax.rsqrt(ms + EPS) * g


def _sigmoid(x):
    return 1.0 / (1.0 + jnp.exp(-x))


def _inproj_kernel(x_ref, g_ref, w_ref, wif_ref, cw_ref, z_ref, gates_ref, xn_ref, halo_ref,
                   *, tiles_per_seq, n_q, n_qk, n_qkvo, k_scale):
    i = pl.program_id(0)
    j = pl.program_id(1)
    tm = x_ref.shape[0]
    n_v_end = n_qk + (n_qk - n_q)

    @pl.when(jnp.logical_and(i == 0, j == 0))
    def _():
        halo_ref[...] = jnp.zeros_like(halo_ref)

    @pl.when(j == 0)
    def _():
        xn = _rms(x_ref[...], g_ref[...]).astype(BF16)
        xn_ref[...] = xn
        gates_ref[...] = lax.dot_general(xn, wif_ref[...], (((1,), (1,)), ((), ())),
                                         preferred_element_type=F32)

    tn = z_ref.shape[1]
    sub = min(INPROJ_SUB, tn)
    col_blocks = [slice(c, c + sub) for c in range(0, tn, sub)]

    def project(cs):
        return lax.dot_general(xn_ref[...], w_ref[0, cs, :].astype(BF16),
                               (((1,), (1,)), ((), ())), preferred_element_type=F32)

    @pl.when(j < n_qk)
    def _():
        first = i % tiles_per_seq == 0
        scale = jnp.where(j >= n_q, k_scale, 1.0)
        for cs in col_blocks:
            acc = project(cs)
            halo = jnp.where(first, 0.0, halo_ref[j, :, cs])
            ext = jnp.concatenate([halo, acc], axis=0)
            w = cw_ref[:, cs]
            y = w[CONV_K - 1:CONV_K] * acc
            for d in range(1, CONV_K):
                y = y + w[CONV_K - 1 - d:CONV_K - d] * pltpu.roll(ext, d, axis=0)[SUBLANES:]
            halo_ref[j, :, cs] = acc[tm - SUBLANES:]
            y = y * _sigmoid(y) * scale
            z_ref[:, cs] = y.astype(BF16)

    @pl.when(jnp.logical_and(j >= n_v_end, j < n_qkvo))
    def _():
        for cs in col_blocks:
            z_ref[:, cs] = _sigmoid(project(cs)).astype(BF16)

    @pl.when(jnp.logical_or(jnp.logical_and(j >= n_qk, j < n_v_end), j >= n_qkvo))
    def _():
        z_ref[...] = project(slice(None)).astype(BF16)


def _inproj(h, g, w_t, w_if, conv_w, layer, *, seq, dm, dh, col_p, col_g, dp):
    n, d = h.shape
    tm = min(INPROJ_TM, n, seq)
    tn = min(INPROJ_TN, dm)
    cols = 4 * dm + 2 * d + dp
    assert seq % tm == 0 and dm % tn == 0 and (2 * d) % tn == 0 and dp % tn == 0
    assert col_p % SUBLANES == 0 and col_g % SUBLANES == 0
    n_q = dm // tn
    n_qkvo = 4 * n_q
    n_gate = (2 * d) // tn
    cw = jnp.zeros((SUBLANES, 2 * dm), F32).at[:CONV_K].set(conv_w)

    def src_row(j):
        row = jnp.where(j < n_qkvo, j * tn,
                        jnp.where(j < n_qkvo + n_gate, col_g + (j - n_qkvo) * tn,
                                  col_p + (j - n_qkvo - n_gate) * tn))
        return pl.multiple_of(row, SUBLANES)

    return pl.pallas_call(
        functools.partial(_inproj_kernel, tiles_per_seq=seq // tm, n_q=n_q, n_qk=2 * n_q,
                          n_qkvo=n_qkvo, k_scale=dh ** -0.5),
        grid=(n // tm, cols // tn),
        in_specs=[
            pl.BlockSpec((tm, d), lambda i, j: (i, 0)),
            pl.BlockSpec((1, d), lambda i, j: (0, 0)),
            pl.BlockSpec((pl.Element(1), pl.Element(tn), pl.Element(d)),
                         lambda i, j: (layer, src_row(j), 0)),
            pl.BlockSpec((None, LANES, d), lambda i, j: (layer, 0, 0)),
            pl.BlockSpec((SUBLANES, tn), lambda i, j: (0, jnp.minimum(j, 2 * n_q - 1))),
        ],
        out_specs=[
            pl.BlockSpec((tm, tn), lambda i, j: (i, j)),
            pl.BlockSpec((tm, LANES), lambda i, j: (i, 0)),
        ],
        out_shape=[
            jax.ShapeDtypeStruct((n, cols), BF16),
            jax.ShapeDtypeStruct((n, LANES), F32),
        ],
        scratch_shapes=[pltpu.VMEM((tm, d), BF16),
                        pltpu.VMEM((2 * n_q, SUBLANES, tn), F32)],
        compiler_params=_cparams(2),
        name="inproj",
    )(h, g.reshape(1, d), w_t, w_if, cw)


def _mlstm_kernel(q_ref, k_ref, v_ref, og_ref, gt_ref, gb_ref, ng_ref,
                  out_ref, ct_ref, n_ref, m_ref, *, chunk, nheads):
    t_rows, dm = q_ref.shape
    dh = dm // nheads

    @pl.when(pl.program_id(1) == 0)
    def _():
        ct_ref[...] = jnp.zeros_like(ct_ref)
        n_ref[...] = jnp.zeros_like(n_ref)
        m_ref[...] = jnp.zeros_like(m_ref)

    gts = gt_ref[...] + gb_ref[...]
    lf = jnp.minimum(gts, 0.0) - jnp.log(1.0 + jnp.exp(-jnp.abs(gts)))
    in_chunk = lax.broadcasted_iota(I32, gts.shape, 0) & (chunk - 1)
    b = lf
    d = 1
    while d < chunk:
        b = b + jnp.where(in_chunk >= d, pltpu.roll(b, d, axis=0), 0.0)
        d *= 2
    gts_t = gts.T
    b_t = b.T

    tt = lax.broadcasted_iota(I32, (chunk, chunk), 0)
    ss = lax.broadcasted_iota(I32, (chunk, chunk), 1)
    causal = tt >= ss

    for hd in range(nheads):
        cs = slice(hd * dh, (hd + 1) * dh)
        icol = gts[:, hd:hd + 1]
        bcol = b[:, nheads + hd:nheads + hd + 1]
        irow = gts_t[hd:hd + 1, :]
        brow = b_t[nheads + hd:nheads + hd + 1, :]
        ct = ct_ref[hd]
        nvec = n_ref[hd]
        m_prev = m_ref[:, hd:hd + 1]
        ng = ng_ref[:, cs]
        for c in range(t_rows // chunk):
            lo, hi = c * chunk, (c + 1) * chunk
            qb, kb = q_ref[lo:hi, cs], k_ref[lo:hi, cs]
            qc, kc = qb.astype(F32), kb.astype(F32)
            vc = v_ref[lo:hi, cs]
            bc, ic = bcol[lo:hi], icol[lo:hi]
            br, ir = brow[:, lo:hi], irow[:, lo:hi]
            dmat = jnp.where(causal, bc - br + ir, NEG_BIG)
            inter = bc + m_prev
            mt = jnp.maximum(inter, jnp.max(dmat, axis=1, keepdims=True))
            wi = jnp.exp(inter - mt)
            s = lax.dot_general(qb, kb, (((1,), (1,)), ((), ())),
                                preferred_element_type=F32) * jnp.exp(dmat - mt)
            num = wi * jnp.dot(qb, ct.astype(BF16), preferred_element_type=F32)
            num = num + jnp.dot(s.astype(BF16), vc, preferred_element_type=F32)
            den = wi * jnp.sum(qc * nvec, axis=1, keepdims=True)
            den = den + jnp.sum(s, axis=1, keepdims=True)
            hr = num / jnp.maximum(jnp.abs(den), jnp.exp(-mt))
            hn = _rms(hr, ng)
            out_ref[lo:hi, cs] = (og_ref[lo:hi, cs].astype(F32) * hn).astype(BF16)
            bl = bc[chunk - 1:chunk]
            gcol = bl - bc + ic
            m_new = jnp.maximum(bl + m_prev, jnp.max(gcol, axis=0, keepdims=True))
            a = jnp.exp(bl + m_prev - m_new)
            kw = kc * jnp.exp(gcol - m_new)
            ct = a * ct + lax.dot_general(kw.astype(BF16), vc, (((0,), (0,)), ((), ())),
                                          preferred_element_type=F32)
            nvec = a * nvec + jnp.sum(kw, axis=0, keepdims=True)
            m_prev = m_new
        ct_ref[hd] = ct
        n_ref[hd] = nvec
        m_ref[:, hd:hd + 1] = m_prev


def _mlstm(z, gates, gate_b, norm_g, *, batch, seq, nheads, dh):
    n = z.shape[0]
    t = min(MLSTM_T, seq)
    chunk = min(MLSTM_CHUNK, t)
    spb = seq // t
    dm = nheads * dh
    gb = jnp.zeros((1, LANES), F32).at[0, :2 * nheads].set(gate_b)

    def zspec(col_block):
        return pl.BlockSpec((t, dm), lambda b, s: (b * spb + s, col_block))

    return pl.pallas_call(
        functools.partial(_mlstm_kernel, chunk=chunk, nheads=nheads),
        grid=(batch, spb),
        in_specs=[
            zspec(0), zspec(1), zspec(2), zspec(3),
            pl.BlockSpec((t, LANES), lambda b, s: (b * spb + s, 0)),
            pl.BlockSpec((1, LANES), lambda b, s: (0, 0)),
            pl.BlockSpec((1, dm), lambda b, s: (0, 0)),
        ],
        out_specs=pl.BlockSpec((t, dm), lambda b, s: (b * spb + s, 0)),
        out_shape=jax.ShapeDtypeStruct((n, dm), BF16),
        scratch_shapes=[
            pltpu.VMEM((nheads, dh, dh), F32),
            pltpu.VMEM((nheads, 1, dh), F32),
            pltpu.VMEM((1, LANES), F32),
        ],
        compiler_params=_cparams(2),
        name="mlstm",
    )(z, z, z, z, gates, gb, norm_g.reshape(1, dm))


def _pool_mix(p_ref, halo_ref, w_ref, sc_ref, pos0):
    t_rows = p_ref.shape[0]
    grp = w_ref.shape[1]
    p = p_ref[...].astype(F32)
    halo = jnp.where(pos0 == 0, 0.0, halo_ref[...].astype(F32))
    ext = jnp.concatenate([halo, p], axis=0)
    pos = pos0 + lax.broadcasted_iota(I32, (t_rows, 1), 0)
    posf = (pos + 1).astype(F32)
    outs = []
    for j, w in enumerate(POOL_WINDOWS):
        cs = slice(j * grp, (j + 1) * grp)
        win = ext[:, cs]
        d = 1
        while d < w:
            win = win + pltpu.roll(win, d, axis=0)
            d *= 2
        pooled = win[POOL_HALO:] / jnp.minimum(posf, float(w)) - p[:, cs]
        y = jnp.dot(pooled.astype(BF16), w_ref[j], preferred_element_type=F32)
        outs.append((y * sc_ref[:, cs]).astype(BF16))
    return jnp.concatenate(outs, axis=1)


def _route(xn, wh, wl, rb, nexp):
    xh = xn.astype(BF16)
    xhf = xh.astype(F32)
    xl = (xn - xhf).astype(BF16)
    half = xn.shape[1] // 2
    lo = lax.shift_right_logical(lax.bitcast_convert_type(xhf[:, :half], I32), 16)
    hi = lax.bitcast_convert_type(xhf[:, half:], I32) & HI16
    logits = (jnp.dot(xh, wh, preferred_element_type=F32)
              + jnp.dot(xh, wl, preferred_element_type=F32)
              + jnp.dot(xl, wh, preferred_element_type=F32)) + rb
    lane = lax.broadcasted_iota(I32, logits.shape, 1)
    logits = jnp.where(lane < nexp, logits, NEG_BIG)
    v1 = jnp.max(logits, axis=1, keepdims=True)
    i1 = jnp.min(jnp.where(logits == v1, lane, LANES), axis=1, keepdims=True)
    rest = jnp.where(lane == i1, NEG_BIG, logits)
    v2 = jnp.max(rest, axis=1, keepdims=True)
    i2 = jnp.min(jnp.where(rest == v2, lane, LANES), axis=1, keepdims=True)
    e2 = jnp.exp(v2 - v1)
    g1 = 1.0 / (1.0 + e2)
    g2 = e2 / (1.0 + e2)
    out = jnp.where(lane == 0, i1.astype(F32), 0.0)
    out = jnp.where(lane == 1, i2.astype(F32), out)
    out = jnp.where(lane == 2, g1, out)
    out = jnp.where(lane == 3, g2, out)
    return out, hi | lo


def _merge_kernel(*refs, tiles_per_seq, nexp):
    (h_ref, hm_ref, p_ref, halo_ref, gm_ref, gp_ref, pw_ref, psc_ref,
     wbm_ref, wbp_ref, wo_ref) = refs[:11]
    tm = h_ref.shape[0]
    pos0 = (pl.program_id(0) % tiles_per_seq) * tm
    hp = _pool_mix(p_ref, halo_ref, pw_ref, psc_ref, pos0)
    hm = hm_ref[...]
    d = h_ref.shape[1]
    sub = min(MERGE_SUB, d)
    out = h_ref[...]
    for c in range(0, d, sub):
        cs = slice(c, c + sub)
        ym = jnp.dot(hm, wbm_ref[:, cs], preferred_element_type=F32)
        yp = jnp.dot(hp, wbp_ref[:, cs], preferred_element_type=F32)
        y = (_sigmoid(gm_ref[:, cs].astype(F32)) * ym
             + _sigmoid(gp_ref[:, cs].astype(F32)) * yp)
        out = out + jnp.dot(y.astype(BF16), wo_ref[cs, :], preferred_element_type=F32)
    if nexp is None:
        refs[11][...] = out
    else:
        n2g_ref, rwh_ref, rwl_ref, rb_ref, out_ref, rt_ref, xp_ref = refs[11:]
        out_ref[...] = out
        rt_ref[...], xp_ref[...] = _route(_rms(out, n2g_ref[...]), rwh_ref[...],
                                          rwl_ref[...], rb_ref[...], nexp)


def _merge(h, hm, z, pool_w, pool_scale, wbm, wbp, wo, layer, router, *,
           seq, gate_block, pool_block):
    n, d = h.shape
    dm = hm.shape[1]
    ngrp, grp, _ = pool_w.shape
    dp = ngrp * grp
    tm = min(MERGE_TM, n, seq)
    assert seq % tm == 0 and tm % POOL_HALO == 0
    hb = tm // POOL_HALO
    once = pl.Buffered(1)
    in_specs = [
        pl.BlockSpec((tm, d), lambda i: (i, 0)),
        pl.BlockSpec((tm, dm), lambda i: (i, 0)),
        pl.BlockSpec((tm, dp), lambda i: (i, pool_block)),
        pl.BlockSpec((POOL_HALO, dp), lambda i: (jnp.maximum(i * hb - 1, 0), pool_block)),
        pl.BlockSpec((tm, d), lambda i: (i, gate_block)),
        pl.BlockSpec((tm, d), lambda i: (i, gate_block + 1)),
        pl.BlockSpec((ngrp, grp, grp), lambda i: (0, 0, 0)),
        pl.BlockSpec((1, dp), lambda i: (0, 0)),
        pl.BlockSpec((None, dm, d), lambda i: (layer, 0, 0), pipeline_mode=once),
        pl.BlockSpec((None, dp, d), lambda i: (layer, 0, 0), pipeline_mode=once),
        pl.BlockSpec((None, d, d), lambda i: (layer, 0, 0), pipeline_mode=once),
    ]
    args = [h, hm, z, z, z, z, pool_w, pool_scale.reshape(1, dp), wbm, wbp, wo]
    out_specs = [pl.BlockSpec((tm, d), lambda i: (i, 0))]
    out_shape = [jax.ShapeDtypeStruct((n, d), F32)]
    nexp = None
    if router is not None:
        g2, router_w, router_b = router
        nexp = router_w.shape[1]
        w = jnp.zeros((d, LANES), F32).at[:, :nexp].set(router_w)
        wh = w.astype(BF16)
        wl = (w - wh.astype(F32)).astype(BF16)
        rb = jnp.zeros((1, LANES), F32).at[0, :nexp].set(router_b)
        in_specs += [
            pl.BlockSpec((1, d), lambda i: (0, 0)),
            pl.BlockSpec((d, LANES), lambda i: (0, 0)),
            pl.BlockSpec((d, LANES), lambda i: (0, 0)),
            pl.BlockSpec((1, LANES), lambda i: (0, 0)),
        ]
        args += [g2.reshape(1, d), wh, wl, rb]
        out_specs += [pl.BlockSpec((tm, LANES), lambda i: (i, 0)),
                      pl.BlockSpec((tm, d // 2), lambda i: (i, 0))]
        out_shape += [jax.ShapeDtypeStruct((n, LANES), F32),
                      jax.ShapeDtypeStruct((n, d // 2), I32)]
    return pl.pallas_call(
        functools.partial(_merge_kernel, tiles_per_seq=seq // tm, nexp=nexp),
        grid=(n // tm,),
        in_specs=in_specs,
        out_specs=out_specs,
        out_shape=out_shape,
        compiler_params=_cparams(1, VMEM_LIMIT_MERGE),
        name="merge",
    )(*args)


def _swiglu_step(xn_ref, wg_ref, wu_ref, wd_ref, out_ref, rows=slice(None)):
    xn = xn_ref[rows, :]
    tf = wg_ref.shape[1]
    sub = min(FFN_SUB, tf)
    for c in range(0, tf, sub):
        cs = slice(c, c + sub)
        a = jnp.dot(xn, wg_ref[:, cs].astype(BF16), preferred_element_type=F32)
        u = jnp.dot(xn, wu_ref[:, cs].astype(BF16), preferred_element_type=F32)
        act = (a * _sigmoid(a) * u).astype(BF16)
        out_ref[rows, :] += jnp.dot(act, wd_ref[cs, :].astype(BF16),
                                    preferred_element_type=F32)


def _ffn_kernel(x_hbm, g_ref, wg_ref, wu_ref, wd_ref, out_ref, xn_ref, sem):
    tm = out_ref.shape[0]

    @pl.when(pl.program_id(1) == 0)
    def _():
        rows = pl.ds(pl.multiple_of(pl.program_id(0) * tm, tm), tm)
        copy = pltpu.make_async_copy(x_hbm.at[rows], out_ref, sem)
        copy.start()
        copy.wait()
        xn_ref[...] = _rms(out_ref[...], g_ref[...]).astype(BF16)

    _swiglu_step(xn_ref, wg_ref, wu_ref, wd_ref, out_ref)


def _ffn(h, g, wg, wu, wd, j):
    n, d = h.shape
    ff = wg.shape[2]
    tm = min(FFN_TM, n)
    tf = min(FFN_TF, ff)
    return pl.pallas_call(
        _ffn_kernel,
        grid=(n // tm, ff // tf),
        in_specs=[
            pl.BlockSpec(memory_space=pl.ANY),
            pl.BlockSpec((1, d), lambda i, f: (0, 0)),
            pl.BlockSpec((None, d, tf), lambda i, f: (j, 0, f)),
            pl.BlockSpec((None, d, tf), lambda i, f: (j, 0, f)),
            pl.BlockSpec((None, tf, d), lambda i, f: (j, f, 0)),
        ],
        out_specs=pl.BlockSpec((tm, d), lambda i, f: (i, 0)),
        out_shape=jax.ShapeDtypeStruct((n, d), F32),
        scratch_shapes=[pltpu.VMEM((tm, d), BF16), pltpu.SemaphoreType.DMA(())],
        compiler_params=_cparams(2),
        name="ffn",
    )(h, g.reshape(1, d), wg, wu, wd)


def _gffn_kernel(te_ref, nv_ref, cnt_ref, src_ref, xp_hbm, wg_ref, wu_ref, wd_ref, out_ref,
                 xg_ref, xn_ref, sem, *, tm, sub, issue_steps):
    t = pl.program_id(0)
    f = pl.program_id(1)
    nf = pl.num_programs(1)
    nv = nv_ref[0]
    cnt = cnt_ref[t]
    half = xg_ref.shape[2]
    rows_per_step = tm // issue_steps
    weights = (wg_ref, wu_ref, wd_ref)

    def slot_copy(slot):
        return pltpu.make_async_copy(xp_hbm.at[pl.ds(0, tm)], xg_ref.at[slot], sem.at[slot])

    def row_copy(tile, slot, r):
        tok = src_ref[tile * tm + r]
        return pltpu.make_async_copy(xp_hbm.at[pl.ds(tok, 1)],
                                     xg_ref.at[slot, pl.ds(r, 1)], sem.at[slot])

    def prefetch_rows():
        nxt = jnp.minimum(t + 1, nv - 1)
        slot = (t + 1) % 2
        r0 = f * rows_per_step
        for r in range(rows_per_step):
            row_copy(nxt, slot, r0 + r).start()

    @pl.when(t < nv)
    def _():
        @pl.when(f == 0)
        def _():
            @pl.when(t == 0)
            def _():
                def body(r, carry):
                    row_copy(0, 0, r).start()
                    return carry

                lax.fori_loop(0, tm, body, 0, unroll=DMA_UNROLL)

            slot_copy(t % 2).wait()
            p = xg_ref[t % 2]
            xn_ref[:, :half] = lax.bitcast_convert_type(lax.shift_left(p, 16), F32).astype(BF16)
            xn_ref[:, half:] = lax.bitcast_convert_type(p & HI16, F32).astype(BF16)
            out_ref[...] = jnp.zeros_like(out_ref)

        full = cnt == tm
        issuing = f < issue_steps

        @pl.when(jnp.logical_and(full, issuing))
        def _():
            prefetch_rows()
            _swiglu_step(xn_ref, *weights, out_ref)

        @pl.when(jnp.logical_and(full, jnp.logical_not(issuing)))
        def _():
            _swiglu_step(xn_ref, *weights, out_ref)

        @pl.when(jnp.logical_not(full))
        def _():
            @pl.when(issuing)
            def _():
                prefetch_rows()

            for q in range(tm // sub):
                @pl.when(q * sub < cnt)
                def _():
                    _swiglu_step(xn_ref, *weights, out_ref, rows=slice(q * sub, (q + 1) * sub))

        @pl.when(jnp.logical_and(t == nv - 1, f == nf - 1))
        def _():
            slot_copy((t + 1) % 2).wait()

    @pl.when(jnp.logical_and(t >= nv, f == 0))
    def _():
        out_ref[...] = jnp.zeros_like(out_ref)


def _grouped_ffn(xp, wg, wu, wd, j, tile_expert, n_valid, tile_rows, src_token, *, tm):
    half = xp.shape[1]
    d = 2 * half
    slots = src_token.shape[0]
    ff = wg.shape[3]
    tf = min(FFN_TF, ff)
    nf = ff // tf
    ntiles = slots // tm
    sub = min(MOE_SUB, tm)
    issue_steps = 1
    while issue_steps * 2 <= nf and tm % (issue_steps * 2) == 0:
        issue_steps *= 2

    def fcol(t, f, nv):
        return jnp.where(t < nv[0], f, nf - 1)

    grid_spec = pltpu.PrefetchScalarGridSpec(
        num_scalar_prefetch=4,
        grid=(ntiles, nf),
        in_specs=[
            pl.BlockSpec(memory_space=pl.ANY),
            pl.BlockSpec((None, None, d, tf),
                         lambda t, f, te, nv, cnt, src: (j, te[t], 0, fcol(t, f, nv))),
            pl.BlockSpec((None, None, d, tf),
                         lambda t, f, te, nv, cnt, src: (j, te[t], 0, fcol(t, f, nv))),
            pl.BlockSpec((None, None, tf, d),
                         lambda t, f, te, nv, cnt, src: (j, te[t], fcol(t, f, nv), 0)),
        ],
        out_specs=pl.BlockSpec((tm, d), lambda t, f, te, nv, cnt, src: (t, 0),
                               pipeline_mode=pl.Buffered(1)),
        scratch_shapes=[
            pltpu.VMEM((2, tm, half), I32),
            pltpu.VMEM((tm, d), BF16),
            pltpu.SemaphoreType.DMA((2,)),
        ],
    )
    return pl.pallas_call(
        functools.partial(_gffn_kernel, tm=tm, sub=sub, issue_steps=issue_steps),
        grid_spec=grid_spec,
        out_shape=jax.ShapeDtypeStruct((slots, d), F32),
        compiler_params=_cparams(2),
        name="grouped_ffn",
    )(tile_expert, n_valid, tile_rows, src_token, xp, wg, wu, wd)


def _combine_kernel(slot_ref, h_ref, rt_ref, g_ref, ys_hbm, out_ref, buf_ref, sem,
                    *, tm, ntok, final_norm):
    i = pl.program_id(0)
    nt = pl.num_programs(0)

    def row_copy(tile, slot, kk, r):
        sl = slot_ref[kk * ntok + tile * tm + r]
        return pltpu.make_async_copy(ys_hbm.at[pl.ds(sl, 1)],
                                     buf_ref.at[slot, kk, pl.ds(r, 1)], sem.at[slot])

    def issue(tile, slot):
        def body(r, carry):
            for kk in range(TOP_K):
                row_copy(tile, slot, kk, r).start()
            return carry

        lax.fori_loop(0, tm, body, 0, unroll=DMA_UNROLL)

    @pl.when(i == 0)
    def _():
        issue(0, 0)

    @pl.when(i + 1 < nt)
    def _():
        issue(i + 1, (i + 1) % 2)

    slot = i % 2
    for kk in range(TOP_K):
        pltpu.make_async_copy(ys_hbm.at[pl.ds(0, tm)], buf_ref.at[slot, kk], sem.at[slot]).wait()
    rt = rt_ref[...]
    out = h_ref[...] + rt[:, 2:3] * buf_ref[slot, 0] + rt[:, 3:4] * buf_ref[slot, 1]
    if final_norm:
        out = _rms(out, g_ref[...])
    out_ref[...] = out


def _combine(h, routing, ys, slot_of, final_g, *, tm):
    n, d = h.shape
    g = jnp.ones((1, d), F32) if final_g is None else final_g.reshape(1, d)
    grid_spec = pltpu.PrefetchScalarGridSpec(
        num_scalar_prefetch=1,
        grid=(n // tm,),
        in_specs=[
            pl.BlockSpec((tm, d), lambda i, sl: (i, 0)),
            pl.BlockSpec((tm, LANES), lambda i, sl: (i, 0)),
            pl.BlockSpec((1, d), lambda i, sl: (0, 0)),
            pl.BlockSpec(memory_space=pl.ANY),
        ],
        out_specs=pl.BlockSpec((tm, d), lambda i, sl: (i, 0)),
        scratch_shapes=[pltpu.VMEM((2, TOP_K, tm, d), F32), pltpu.SemaphoreType.DMA((2,))],
    )
    return pl.pallas_call(
        functools.partial(_combine_kernel, tm=tm, ntok=n, final_norm=final_g is not None),
        grid_spec=grid_spec,
        out_shape=jax.ShapeDtypeStruct((n, d), F32),
        compiler_params=_cparams(1),
        name="combine",
    )(slot_of, h, routing, g, ys)


def _moe(h, routing, xp, nexp, wg, wu, wd, j, final_g):
    n, d = h.shape
    tm = min(MOE_TM, n)
    experts = routing[:, :TOP_K].astype(I32).T.reshape(-1)
    onehot = (experts[:, None] == jnp.arange(nexp, dtype=I32)[None, :]).astype(I32)
    rank = jnp.sum((jnp.cumsum(onehot, axis=0) - onehot) * onehot, axis=1)
    counts = jnp.sum(onehot, axis=0)
    tiles_per = (counts + tm - 1) // tm
    tile_end = jnp.cumsum(tiles_per)
    tile_start = tile_end - tiles_per
    slot_of = (tile_start[experts] * tm + rank).astype(I32)
    max_tiles = (TOP_K * n) // tm + nexp
    tokens = jnp.tile(jnp.arange(n, dtype=I32), TOP_K)
    src_token = jnp.zeros((max_tiles * tm,), I32).at[slot_of].set(tokens)
    n_valid = tile_end[-1:].astype(I32)
    tile_ids = jnp.arange(max_tiles, dtype=I32)
    tile_expert = jnp.minimum(
        jnp.sum((tile_ids[:, None] >= tile_end[None, :]).astype(I32), axis=1), nexp - 1)
    tile_rows = jnp.clip(counts[tile_expert] - (tile_ids - tile_start[tile_expert]) * tm,
                         0, tm).astype(I32)
    ys = _grouped_ffn(xp, wg, wu, wd, j, tile_expert, n_valid, tile_rows, src_token, tm=tm)
    return _combine(h, routing, ys, slot_of, final_g, tm=min(COMBINE_TM, n))


def _norm_kernel(x_ref, g_ref, out_ref):
    out_ref[...] = _rms(x_ref[...], g_ref[...])


def _final_norm(h, g):
    n, d = h.shape
    tm = min(NORM_TM, n)
    return pl.pallas_call(
        _norm_kernel,
        grid=(n // tm,),
        in_specs=[pl.BlockSpec((tm, d), lambda i: (i, 0)),
                  pl.BlockSpec((1, d), lambda i: (0, 0))],
        out_specs=pl.BlockSpec((tm, d), lambda i: (i, 0)),
        out_shape=jax.ShapeDtypeStruct((n, d), F32),
        compiler_params=_cparams(1),
        name="final_norm",
    )(h, g.reshape(1, d))


def kernel(x, norm1_g, norm2_g, final_g, w_in, mlstm_gate_b, conv_w, mlstm_norm_g, pool_w, pool_scale, w_branch_m, w_branch_p, w_out, ffn_w_gate, ffn_w_up, ffn_w_down, router_w, router_b, exp_w_gate, exp_w_up, exp_w_down):
    batch, seq, d = x.shape
    depth = w_in.shape[0]
    nheads = mlstm_gate_b.shape[1] // 2
    dm = conv_w.shape[2] // 2
    dh = dm // nheads
    dp = pool_scale.shape[1]
    col_o_end = 4 * dm
    col_p = col_o_end + 2 * nheads
    col_g = col_p + dp
    assert col_o_end % d == 0 and (col_o_end + 2 * d) % dp == 0
    w_t = jnp.swapaxes(w_in, 1, 2)
    w_if = jnp.zeros((depth, LANES, d), F32).at[:, :2 * nheads].set(
        w_t[:, col_o_end:col_p]).astype(BF16)
    gate_block = col_o_end // d
    pool_block = (col_o_end + 2 * d) // dp
    wbm = w_branch_m.astype(BF16)
    wbp = w_branch_p.astype(BF16)
    wo = w_out.astype(BF16)
    pw = pool_w.astype(BF16)

    h = x.reshape(batch * seq, d)
    for l in range(depth):
        z, gates = _inproj(h, norm1_g[l], w_t, w_if, conv_w[l], l, seq=seq, dm=dm, dh=dh,
                           col_p=col_p, col_g=col_g, dp=dp)
        hm = _mlstm(z, gates, mlstm_gate_b[l], mlstm_norm_g[l],
                    batch=batch, seq=seq, nheads=nheads, dh=dh)
        j = l // 2
        router = None if l % 2 == 0 else (norm2_g[l], router_w[j], router_b[j])
        merged = _merge(h, hm, z, pw[l], pool_scale[l], wbm, wbp, wo, l, router,
                        seq=seq, gate_block=gate_block, pool_block=pool_block)
        if l % 2 == 0:
            h = _ffn(merged[0], norm2_g[l], ffn_w_gate, ffn_w_up, ffn_w_down, j)
        else:
            fg = final_g if l == depth - 1 else None
            h, routing, xp = merged
            h = _moe(h, routing, xp, router_w.shape[2], exp_w_gate, exp_w_up, exp_w_down, j, fg)
    if depth % 2 == 1:
        h = _final_norm(h, final_g)
    return h.reshape(batch, seq, d)
```

```python
import functools

import jax
import jax.numpy as jnp
from jax import lax
from jax.experimental import pallas as pl
from jax.experimental.pallas import tpu as pltpu

F32 = jnp.float32
BF16 = jnp.bfloat16
I32 = jnp.int32

EPS = 1e-6
CONV_K = 4
POOL_WINDOWS = (2, 4, 8, 16)
POOL_HALO = 16
TOP_K = 2
LANES = 128
SUBLANES = 8
NEG_BIG = -1e30
HI16 = -65536
VMEM_LIMIT = 56 * 1024 * 1024
VMEM_LIMIT_MERGE = 59 * 1024 * 1024

INPROJ_TM = 1024
INPROJ_TN = 1024
INPROJ_SUB = 256
MLSTM_T = 1024
MLSTM_CHUNK = 256
MERGE_TM = 512
MERGE_SUB = 512
FFN_TM = 1024
FFN_TF = 512
FFN_SUB = 256
MOE_TM = 1024
MOE_SUB = 256
COMBINE_TM = 256
NORM_TM = 1024
DMA_UNROLL = 8


def _cparams(n_axes, vmem_limit=VMEM_LIMIT):
    return pltpu.CompilerParams(
        dimension_semantics=("arbitrary",) * n_axes,
        vmem_limit_bytes=vmem_limit,
    )


def _rms(x, g):
    ms = jnp.mean(x * x, axis=-1, keepdims=True)
    return x * lax.rsqrt(ms + EPS) * g


def _sigmoid(x):
    return 1.0 / (1.0 + jnp.exp(-x))


def _inproj_kernel(x_ref, g_ref, w_ref, wif_ref, cw_ref, z_ref, gates_ref, xn_ref, halo_ref,
                   *, tiles_per_seq, n_q, n_qk, n_qkvo, k_scale):
    i = pl.program_id(0)
    j = pl.program_id(1)
    tm = x_ref.shape[0]
    n_v_end = n_qk + (n_qk - n_q)

    @pl.when(jnp.logical_and(i == 0, j == 0))
    def _():
        halo_ref[...] = jnp.zeros_like(halo_ref)

    @pl.when(j == 0)
    def _():
        xn = _rms(x_ref[...], g_ref[...]).astype(BF16)
        xn_ref[...] = xn
        gates_ref[...] = lax.dot_general(xn, wif_ref[...], (((1,), (1,)), ((), ())),
                                         preferred_element_type=F32)

    tn = z_ref.shape[1]
    sub = min(INPROJ_SUB, tn)
    col_blocks = [slice(c, c + sub) for c in range(0, tn, sub)]

    def project(cs):
        return lax.dot_general(xn_ref[...], w_ref[0, cs, :].astype(BF16),
                               (((1,), (1,)), ((), ())), preferred_element_type=F32)

    @pl.when(j < n_qk)
    def _():
        first = i % tiles_per_seq == 0
        scale = jnp.where(j >= n_q, k_scale, 1.0)
        for cs in col_blocks:
            acc = project(cs)
            halo = jnp.where(first, 0.0, halo_ref[j, :, cs])
            ext = jnp.concatenate([halo, acc], axis=0)
            w = cw_ref[:, cs]
            y = w[CONV_K - 1:CONV_K] * acc
            for d in range(1, CONV_K):
                y = y + w[CONV_K - 1 - d:CONV_K - d] * pltpu.roll(ext, d, axis=0)[SUBLANES:]
            halo_ref[j, :, cs] = acc[tm - SUBLANES:]
            y = y * _sigmoid(y) * scale
            z_ref[:, cs] = y.astype(BF16)

    @pl.when(jnp.logical_and(j >= n_v_end, j < n_qkvo))
    def _():
        for cs in col_blocks:
            z_ref[:, cs] = _sigmoid(project(cs)).astype(BF16)

    @pl.when(jnp.logical_or(jnp.logical_and(j >= n_qk, j < n_v_end), j >= n_qkvo))
    def _():
        z_ref[...] = project(slice(None)).astype(BF16)


def _inproj(h, g, w_t, w_if, conv_w, layer, *, seq, dm, dh, col_p, col_g, dp):
    n, d = h.shape
    tm = min(INPROJ_TM, n, seq)
    tn = min(INPROJ_TN, dm)
    cols = 4 * dm + 2 * d + dp
    assert seq % tm == 0 and dm % tn == 0 and (2 * d) % tn == 0 and dp % tn == 0
    assert col_p % SUBLANES == 0 and col_g % SUBLANES == 0
    n_q = dm // tn
    n_qkvo = 4 * n_q
    n_gate = (2 * d) // tn
    cw = jnp.zeros((SUBLANES, 2 * dm), F32).at[:CONV_K].set(conv_w)

    def src_row(j):
        row = jnp.where(j < n_qkvo, j * tn,
                        jnp.where(j < n_qkvo + n_gate, col_g + (j - n_qkvo) * tn,
                                  col_p + (j - n_qkvo - n_gate) * tn))
        return pl.multiple_of(row, SUBLANES)

    return pl.pallas_call(
        functools.partial(_inproj_kernel, tiles_per_seq=seq // tm, n_q=n_q, n_qk=2 * n_q,
                          n_qkvo=n_qkvo, k_scale=dh ** -0.5),
        grid=(n // tm, cols // tn),
        in_specs=[
            pl.BlockSpec((tm, d), lambda i, j: (i, 0)),
            pl.BlockSpec((1, d), lambda i, j: (0, 0)),
            pl.BlockSpec((pl.Element(1), pl.Element(tn), pl.Element(d)),
                         lambda i, j: (layer, src_row(j), 0)),
            pl.BlockSpec((None, LANES, d), lambda i, j: (layer, 0, 0)),
            pl.BlockSpec((SUBLANES, tn), lambda i, j: (0, jnp.minimum(j, 2 * n_q - 1))),
        ],
        out_specs=[
            pl.BlockSpec((tm, tn), lambda i, j: (i, j)),
            pl.BlockSpec((tm, LANES), lambda i, j: (i, 0)),
        ],
        out_shape=[
            jax.ShapeDtypeStruct((n, cols), BF16),
            jax.ShapeDtypeStruct((n, LANES), F32),
        ],
        scratch_shapes=[pltpu.VMEM((tm, d), BF16),
                        pltpu.VMEM((2 * n_q, SUBLANES, tn), F32)],
        compiler_params=_cparams(2),
        name="inproj",
    )(h, g.reshape(1, d), w_t, w_if, cw)


def _mlstm_kernel(q_ref, k_ref, v_ref, og_ref, gt_ref, gb_ref, ng_ref,
                  out_ref, ct_ref, m_ref, *, chunk, nheads):
    t_rows, dm = q_ref.shape
    dh = dm // nheads

    @pl.when(pl.program_id(1) == 0)
    def _():
        ct_ref[...] = jnp.zeros_like(ct_ref)
        m_ref[...] = jnp.zeros_like(m_ref)

    gts = gt_ref[...] + gb_ref[...]
    lf = jnp.minimum(gts, 0.0) - jnp.log(1.0 + jnp.exp(-jnp.abs(gts)))
    in_chunk = lax.broadcasted_iota(I32, gts.shape, 0) & (chunk - 1)
    b = lf
    d = 1
    while d < chunk:
        b = b + jnp.where(in_chunk >= d, pltpu.roll(b, d, axis=0), 0.0)
        d *= 2
    u = gts - pltpu.roll(b, LANES - nheads, axis=1)
    d = 1
    while d < chunk:
        u = jnp.maximum(u, jnp.where(in_chunk >= d, pltpu.roll(u, d, axis=0), NEG_BIG))
        d *= 2
    gts_t = gts.T
    b_t = b.T

    tt = lax.broadcasted_iota(I32, (chunk, chunk), 0)
    ss = lax.broadcasted_iota(I32, (chunk, chunk), 1)
    causal = tt >= ss

    def wide(x, reps):
        return jnp.concatenate([x] * reps, axis=1)

    reps = dh // LANES
    ones_blk = jnp.ones((chunk, LANES), BF16)
    mean_blk = jnp.full((dh, LANES), 1.0 / dh, BF16)
    state = [ct_ref[hd] for hd in range(nheads)]
    m_run = [m_ref[hd] for hd in range(nheads)]
    i_rep = [jnp.broadcast_to(gts[:, hd:hd + 1], gts.shape) for hd in range(nheads)]
    b_rep = [jnp.broadcast_to(b[:, nheads + hd:nheads + hd + 1], gts.shape)
             for hd in range(nheads)]
    u_rep = [jnp.broadcast_to(u[:, hd:hd + 1], gts.shape) for hd in range(nheads)]
    for c in range(t_rows // chunk):
        lo, hi = c * chunk, (c + 1) * chunk
        for hd in range(nheads):
            cs = slice(hd * dh, (hd + 1) * dh)
            qb, kb = q_ref[lo:hi, cs], k_ref[lo:hi, cs]
            v_ext = jnp.concatenate([v_ref[lo:hi, cs], ones_blk], axis=1)
            bc, ic, uc = b_rep[hd][lo:hi], i_rep[hd][lo:hi], u_rep[hd][lo:hi]
            br, ir = b_t[nheads + hd:nheads + hd + 1, lo:hi], gts_t[hd:hd + 1, lo:hi]
            m_prev = m_run[hd]
            inter = bc + m_prev
            mt = jnp.maximum(inter, bc + uc)
            wi = jnp.exp(inter - mt)
            dmat = jnp.where(causal, wide(bc, chunk // LANES) - br + ir, NEG_BIG)
            s = lax.dot_general(qb, kb, (((1,), (1,)), ((), ())),
                                preferred_element_type=F32)
            s = s * jnp.exp(dmat - wide(mt, chunk // LANES))
            nd = wide(wi, reps + 1) * jnp.dot(qb, state[hd].astype(BF16),
                                              preferred_element_type=F32)
            nd = nd + jnp.dot(s.astype(BF16), v_ext, preferred_element_type=F32)
            num, den = nd[:, :dh], nd[:, dh:]
            rden = 1.0 / jnp.maximum(jnp.abs(den), jnp.exp(-mt))
            ms = jnp.dot((num * num).astype(BF16), mean_blk, preferred_element_type=F32)
            scale = rden * lax.rsqrt(ms * (rden * rden) + EPS)
            hn = num * wide(scale, reps) * ng_ref[:, cs]
            out_ref[lo:hi, cs] = (og_ref[lo:hi, cs].astype(F32) * hn).astype(BF16)
            bl = bc[chunk - 1:chunk]
            gcol = bl - bc + ic
            m_new = jnp.maximum(bl + m_prev, jnp.max(gcol, axis=0, keepdims=True))
            a = jnp.exp(bl + m_prev - m_new)
            kw = kb.astype(F32) * wide(jnp.exp(gcol - m_new), reps)
            state[hd] = wide(a, reps + 1) * state[hd] + lax.dot_general(
                kw.astype(BF16), v_ext, (((0,), (0,)), ((), ())), preferred_element_type=F32)
            m_run[hd] = m_new
    for hd in range(nheads):
        ct_ref[hd] = state[hd]
        m_ref[hd] = m_run[hd]


def _mlstm(z, gates, gate_b, norm_g, *, batch, seq, nheads, dh):
    n = z.shape[0]
    t = min(MLSTM_T, seq)
    chunk = min(MLSTM_CHUNK, t)
    spb = seq // t
    dm = nheads * dh
    gb = jnp.zeros((1, LANES), F32).at[0, :2 * nheads].set(gate_b)

    def zspec(col_block):
        return pl.BlockSpec((t, dm), lambda b, s: (b * spb + s, col_block))

    return pl.pallas_call(
        functools.partial(_mlstm_kernel, chunk=chunk, nheads=nheads),
        grid=(batch, spb),
        in_specs=[
            zspec(0), zspec(1), zspec(2), zspec(3),
            pl.BlockSpec((t, LANES), lambda b, s: (b * spb + s, 0)),
            pl.BlockSpec((1, LANES), lambda b, s: (0, 0)),
            pl.BlockSpec((1, dm), lambda b, s: (0, 0)),
        ],
        out_specs=pl.BlockSpec((t, dm), lambda b, s: (b * spb + s, 0)),
        out_shape=jax.ShapeDtypeStruct((n, dm), BF16),
        scratch_shapes=[
            pltpu.VMEM((nheads, dh, dh + LANES), F32),
            pltpu.VMEM((nheads, 1, LANES), F32),
        ],
        compiler_params=_cparams(2),
        name="mlstm",
    )(z, z, z, z, gates, gb, norm_g.reshape(1, dm))


def _pool_mix(p_ref, halo_ref, w_ref, sc_ref, pos0):
    t_rows = p_ref.shape[0]
    grp = w_ref.shape[1]
    p = p_ref[...].astype(F32)
    halo = jnp.where(pos0 == 0, 0.0, halo_ref[...].astype(F32))
    ext = jnp.concatenate([halo, p], axis=0)
    pos = pos0 + lax.broadcasted_iota(I32, (t_rows, 1), 0)
    posf = (pos + 1).astype(F32)
    outs = []
    for j, w in enumerate(POOL_WINDOWS):
        cs = slice(j * grp, (j + 1) * grp)
        win = ext[:, cs]
        d = 1
        while d < w:
            win = win + pltpu.roll(win, d, axis=0)
            d *= 2
        pooled = win[POOL_HALO:] / jnp.minimum(posf, float(w)) - p[:, cs]
        y = jnp.dot(pooled.astype(BF16), w_ref[j], preferred_element_type=F32)
        outs.append((y * sc_ref[:, cs]).astype(BF16))
    return jnp.concatenate(outs, axis=1)


def _route(xn, wh, wl, rb, nexp):
    xh = xn.astype(BF16)
    xhf = xh.astype(F32)
    xl = (xn - xhf).astype(BF16)
    half = xn.shape[1] // 2
    lo = lax.shift_right_logical(lax.bitcast_convert_type(xhf[:, :half], I32), 16)
    hi = lax.bitcast_convert_type(xhf[:, half:], I32) & HI16
    logits = (jnp.dot(xh, wh, preferred_element_type=F32)
              + jnp.dot(xh, wl, preferred_element_type=F32)
              + jnp.dot(xl, wh, preferred_element_type=F32)) + rb
    lane = lax.broadcasted_iota(I32, logits.shape, 1)
    logits = jnp.where(lane < nexp, logits, NEG_BIG)
    v1 = jnp.max(logits, axis=1, keepdims=True)
    i1 = jnp.min(jnp.where(logits == v1, lane, LANES), axis=1, keepdims=True)
    rest = jnp.where(lane == i1, NEG_BIG, logits)
    v2 = jnp.max(rest, axis=1, keepdims=True)
    i2 = jnp.min(jnp.where(rest == v2, lane, LANES), axis=1, keepdims=True)
    e2 = jnp.exp(v2 - v1)
    g1 = 1.0 / (1.0 + e2)
    g2 = e2 / (1.0 + e2)
    out = jnp.where(lane == 0, i1.astype(F32), 0.0)
    out = jnp.where(lane == 1, i2.astype(F32), out)
    out = jnp.where(lane == 2, g1, out)
    out = jnp.where(lane == 3, g2, out)
    return out, hi | lo


def _merge_kernel(*refs, tiles_per_seq, nexp):
    (h_ref, hm_ref, p_ref, halo_ref, gm_ref, gp_ref, pw_ref, psc_ref,
     wbm_ref, wbp_ref, wo_ref) = refs[:11]
    tm = h_ref.shape[0]
    pos0 = (pl.program_id(0) % tiles_per_seq) * tm
    hp = _pool_mix(p_ref, halo_ref, pw_ref, psc_ref, pos0)
    hm = hm_ref[...]
    d = h_ref.shape[1]
    sub = min(MERGE_SUB, d)
    out = h_ref[...]
    for c in range(0, d, sub):
        cs = slice(c, c + sub)
        ym = jnp.dot(hm, wbm_ref[:, cs], preferred_element_type=F32)
        yp = jnp.dot(hp, wbp_ref[:, cs], preferred_element_type=F32)
        y = (_sigmoid(gm_ref[:, cs].astype(F32)) * ym
             + _sigmoid(gp_ref[:, cs].astype(F32)) * yp)
        out = out + jnp.dot(y.astype(BF16), wo_ref[cs, :], preferred_element_type=F32)
    if nexp is None:
        refs[11][...] = out
    else:
        n2g_ref, rwh_ref, rwl_ref, rb_ref, out_ref, rt_ref, xp_ref = refs[11:]
        out_ref[...] = out
        rt_ref[...], xp_ref[...] = _route(_rms(out, n2g_ref[...]), rwh_ref[...],
                                          rwl_ref[...], rb_ref[...], nexp)


def _merge(h, hm, z, pool_w, pool_scale, wbm, wbp, wo, layer, router, *,
           seq, gate_block, pool_block):
    n, d = h.shape
    dm = hm.shape[1]
    ngrp, grp, _ = pool_w.shape
    dp = ngrp * grp
    tm = min(MERGE_TM, n, seq)
    assert seq % tm == 0 and tm % POOL_HALO == 0
    hb = tm // POOL_HALO
    once = pl.Buffered(1)
    in_specs = [
        pl.BlockSpec((tm, d), lambda i: (i, 0)),
        pl.BlockSpec((tm, dm), lambda i: (i, 0)),
        pl.BlockSpec((tm, dp), lambda i: (i, pool_block)),
        pl.BlockSpec((POOL_HALO, dp), lambda i: (jnp.maximum(i * hb - 1, 0), pool_block)),
        pl.BlockSpec((tm, d), lambda i: (i, gate_block)),
        pl.BlockSpec((tm, d), lambda i: (i, gate_block + 1)),
        pl.BlockSpec((ngrp, grp, grp), lambda i: (0, 0, 0)),
        pl.BlockSpec((1, dp), lambda i: (0, 0)),
        pl.BlockSpec((None, dm, d), lambda i: (layer, 0, 0), pipeline_mode=once),
        pl.BlockSpec((None, dp, d), lambda i: (layer, 0, 0), pipeline_mode=once),
        pl.BlockSpec((None, d, d), lambda i: (layer, 0, 0), pipeline_mode=once),
    ]
    args = [h, hm, z, z, z, z, pool_w, pool_scale.reshape(1, dp), wbm, wbp, wo]
    out_specs = [pl.BlockSpec((tm, d), lambda i: (i, 0))]
    out_shape = [jax.ShapeDtypeStruct((n, d), F32)]
    nexp = None
    if router is not None:
        g2, router_w, router_b = router
        nexp = router_w.shape[1]
        w = jnp.zeros((d, LANES), F32).at[:, :nexp].set(router_w)
        wh = w.astype(BF16)
        wl = (w - wh.astype(F32)).astype(BF16)
        rb = jnp.zeros((1, LANES), F32).at[0, :nexp].set(router_b)
        in_specs += [
            pl.BlockSpec((1, d), lambda i: (0, 0)),
            pl.BlockSpec((d, LANES), lambda i: (0, 0)),
            pl.BlockSpec((d, LANES), lambda i: (0, 0)),
            pl.BlockSpec((1, LANES), lambda i: (0, 0)),
        ]
        args += [g2.reshape(1, d), wh, wl, rb]
        out_specs += [pl.BlockSpec((tm, LANES), lambda i: (i, 0)),
                      pl.BlockSpec((tm, d // 2), lambda i: (i, 0))]
        out_shape += [jax.ShapeDtypeStruct((n, LANES), F32),
                      jax.ShapeDtypeStruct((n, d // 2), I32)]
    return pl.pallas_call(
        functools.partial(_merge_kernel, tiles_per_seq=seq // tm, nexp=nexp),
        grid=(n // tm,),
        in_specs=in_specs,
        out_specs=out_specs,
        out_shape=out_shape,
        compiler_params=_cparams(1, VMEM_LIMIT_MERGE),
        name="merge",
    )(*args)


def _swiglu_step(xn_ref, wg_ref, wu_ref, wd_ref, out_ref, rows=slice(None)):
    xn = xn_ref[rows, :]
    tf = wg_ref.shape[1]
    sub = min(FFN_SUB, tf)
    for c in range(0, tf, sub):
        cs = slice(c, c + sub)
        a = jnp.dot(xn, wg_ref[:, cs].astype(BF16), preferred_element_type=F32)
        u = jnp.dot(xn, wu_ref[:, cs].astype(BF16), preferred_element_type=F32)
        act = (a * _sigmoid(a) * u).astype(BF16)
        out_ref[rows, :] += jnp.dot(act, wd_ref[cs, :].astype(BF16),
                                    preferred_element_type=F32)


def _ffn_kernel(x_hbm, g_ref, wg_ref, wu_ref, wd_ref, out_ref, xn_ref, sem):
    tm = out_ref.shape[0]

    @pl.when(pl.program_id(1) == 0)
    def _():
        rows = pl.ds(pl.multiple_of(pl.program_id(0) * tm, tm), tm)
        copy = pltpu.make_async_copy(x_hbm.at[rows], out_ref, sem)
        copy.start()
        copy.wait()
        xn_ref[...] = _rms(out_ref[...], g_ref[...]).astype(BF16)

    _swiglu_step(xn_ref, wg_ref, wu_ref, wd_ref, out_ref)


def _ffn(h, g, wg, wu, wd, j):
    n, d = h.shape
    ff = wg.shape[2]
    tm = min(FFN_TM, n)
    tf = min(FFN_TF, ff)
    return pl.pallas_call(
        _ffn_kernel,
        grid=(n // tm, ff // tf),
        in_specs=[
            pl.BlockSpec(memory_space=pl.ANY),
            pl.BlockSpec((1, d), lambda i, f: (0, 0)),
            pl.BlockSpec((None, d, tf), lambda i, f: (j, 0, f)),
            pl.BlockSpec((None, d, tf), lambda i, f: (j, 0, f)),
            pl.BlockSpec((None, tf, d), lambda i, f: (j, f, 0)),
        ],
        out_specs=pl.BlockSpec((tm, d), lambda i, f: (i, 0)),
        out_shape=jax.ShapeDtypeStruct((n, d), F32),
        scratch_shapes=[pltpu.VMEM((tm, d), BF16), pltpu.SemaphoreType.DMA(())],
        compiler_params=_cparams(2),
        name="ffn",
    )(h, g.reshape(1, d), wg, wu, wd)


def _gffn_kernel(te_ref, nv_ref, cnt_ref, src_ref, xp_hbm, wg_ref, wu_ref, wd_ref, out_ref,
                 xg_ref, xn_ref, sem, *, tm, sub, issue_steps):
    t = pl.program_id(0)
    f = pl.program_id(1)
    nf = pl.num_programs(1)
    nv = nv_ref[0]
    cnt = cnt_ref[t]
    half = xg_ref.shape[2]
    rows_per_step = tm // issue_steps
    weights = (wg_ref, wu_ref, wd_ref)

    def slot_copy(slot):
        return pltpu.make_async_copy(xp_hbm.at[pl.ds(0, tm)], xg_ref.at[slot], sem.at[slot])

    def row_copy(tile, slot, r):
        tok = src_ref[tile * tm + r]
        return pltpu.make_async_copy(xp_hbm.at[pl.ds(tok, 1)],
                                     xg_ref.at[slot, pl.ds(r, 1)], sem.at[slot])

    def prefetch_rows():
        nxt = jnp.minimum(t + 1, nv - 1)
        slot = (t + 1) % 2
        r0 = f * rows_per_step
        for r in range(rows_per_step):
            row_copy(nxt, slot, r0 + r).start()

    @pl.when(t < nv)
    def _():
        @pl.when(f == 0)
        def _():
            @pl.when(t == 0)
            def _():
                def body(r, carry):
                    row_copy(0, 0, r).start()
                    return carry

                lax.fori_loop(0, tm, body, 0, unroll=DMA_UNROLL)

            slot_copy(t % 2).wait()
            p = xg_ref[t % 2]
            xn_ref[:, :half] = lax.bitcast_convert_type(lax.shift_left(p, 16), F32).astype(BF16)
            xn_ref[:, half:] = lax.bitcast_convert_type(p & HI16, F32).astype(BF16)
            out_ref[...] = jnp.zeros_like(out_ref)

        full = cnt == tm
        issuing = f < issue_steps

        @pl.when(jnp.logical_and(full, issuing))
        def _():
            prefetch_rows()
            _swiglu_step(xn_ref, *weights, out_ref)

        @pl.when(jnp.logical_and(full, jnp.logical_not(issuing)))
        def _():
            _swiglu_step(xn_ref, *weights, out_ref)

        @pl.when(jnp.logical_not(full))
        def _():
            @pl.when(issuing)
            def _():
                prefetch_rows()

            for q in range(tm // sub):
                @pl.when(q * sub < cnt)
                def _():
                    _swiglu_step(xn_ref, *weights, out_ref, rows=slice(q * sub, (q + 1) * sub))

        @pl.when(jnp.logical_and(t == nv - 1, f == nf - 1))
        def _():
            slot_copy((t + 1) % 2).wait()

    @pl.when(jnp.logical_and(t >= nv, f == 0))
    def _():
        out_ref[...] = jnp.zeros_like(out_ref)


def _grouped_ffn(xp, wg, wu, wd, j, tile_expert, n_valid, tile_rows, src_token, *, tm):
    half = xp.shape[1]
    d = 2 * half
    slots = src_token.shape[0]
    ff = wg.shape[3]
    tf = min(FFN_TF, ff)
    nf = ff // tf
    ntiles = slots // tm
    sub = min(MOE_SUB, tm)
    issue_steps = 1
    while issue_steps * 2 <= nf and tm % (issue_steps * 2) == 0:
        issue_steps *= 2

    def fcol(t, f, nv):
        return jnp.where(t < nv[0], f, nf - 1)

    grid_spec = pltpu.PrefetchScalarGridSpec(
        num_scalar_prefetch=4,
        grid=(ntiles, nf),
        in_specs=[
            pl.BlockSpec(memory_space=pl.ANY),
            pl.BlockSpec((None, None, d, tf),
                         lambda t, f, te, nv, cnt, src: (j, te[t], 0, fcol(t, f, nv))),
            pl.BlockSpec((None, None, d, tf),
                         lambda t, f, te, nv, cnt, src: (j, te[t], 0, fcol(t, f, nv))),
            pl.BlockSpec((None, None, tf, d),
                         lambda t, f, te, nv, cnt, src: (j, te[t], fcol(t, f, nv), 0)),
        ],
        out_specs=pl.BlockSpec((tm, d), lambda t, f, te, nv, cnt, src: (t, 0),
                               pipeline_mode=pl.Buffered(1)),
        scratch_shapes=[
            pltpu.VMEM((2, tm, half), I32),
            pltpu.VMEM((tm, d), BF16),
            pltpu.SemaphoreType.DMA((2,)),
        ],
    )
    return pl.pallas_call(
        functools.partial(_gffn_kernel, tm=tm, sub=sub, issue_steps=issue_steps),
        grid_spec=grid_spec,
        out_shape=jax.ShapeDtypeStruct((slots, d), F32),
        compiler_params=_cparams(2),
        name="grouped_ffn",
    )(tile_expert, n_valid, tile_rows, src_token, xp, wg, wu, wd)


def _combine_kernel(slot_ref, h_ref, rt_ref, g_ref, ys_hbm, out_ref, buf_ref, sem,
                    *, tm, ntok, final_norm):
    i = pl.program_id(0)
    nt = pl.num_programs(0)

    def row_copy(tile, slot, kk, r):
        sl = slot_ref[kk * ntok + tile * tm + r]
        return pltpu.make_async_copy(ys_hbm.at[pl.ds(sl, 1)],
                                     buf_ref.at[slot, kk, pl.ds(r, 1)], sem.at[slot])

    def issue(tile, slot):
        def body(r, carry):
            for kk in range(TOP_K):
                row_copy(tile, slot, kk, r).start()
            return carry

        lax.fori_loop(0, tm, body, 0, unroll=DMA_UNROLL)

    @pl.when(i == 0)
    def _():
        issue(0, 0)

    @pl.when(i + 1 < nt)
    def _():
        issue(i + 1, (i + 1) % 2)

    slot = i % 2
    for kk in range(TOP_K):
        pltpu.make_async_copy(ys_hbm.at[pl.ds(0, tm)], buf_ref.at[slot, kk], sem.at[slot]).wait()
    rt = rt_ref[...]
    out = h_ref[...] + rt[:, 2:3] * buf_ref[slot, 0] + rt[:, 3:4] * buf_ref[slot, 1]
    if final_norm:
        out = _rms(out, g_ref[...])
    out_ref[...] = out


def _combine(h, routing, ys, slot_of, final_g, *, tm):
    n, d = h.shape
    g = jnp.ones((1, d), F32) if final_g is None else final_g.reshape(1, d)
    grid_spec = pltpu.PrefetchScalarGridSpec(
        num_scalar_prefetch=1,
        grid=(n // tm,),
        in_specs=[
            pl.BlockSpec((tm, d), lambda i, sl: (i, 0)),
            pl.BlockSpec((tm, LANES), lambda i, sl: (i, 0)),
            pl.BlockSpec((1, d), lambda i, sl: (0, 0)),
            pl.BlockSpec(memory_space=pl.ANY),
        ],
        out_specs=pl.BlockSpec((tm, d), lambda i, sl: (i, 0)),
        scratch_shapes=[pltpu.VMEM((2, TOP_K, tm, d), F32), pltpu.SemaphoreType.DMA((2,))],
    )
    return pl.pallas_call(
        functools.partial(_combine_kernel, tm=tm, ntok=n, final_norm=final_g is not None),
        grid_spec=grid_spec,
        out_shape=jax.ShapeDtypeStruct((n, d), F32),
        compiler_params=_cparams(1),
        name="combine",
    )(slot_of, h, routing, g, ys)


def _moe(h, routing, xp, nexp, wg, wu, wd, j, final_g):
    n, d = h.shape
    tm = min(MOE_TM, n)
    experts = routing[:, :TOP_K].astype(I32).T.reshape(-1)
    onehot = (experts[:, None] == jnp.arange(nexp, dtype=I32)[None, :]).astype(I32)
    rank = jnp.sum((jnp.cumsum(onehot, axis=0) - onehot) * onehot, axis=1)
    counts = jnp.sum(onehot, axis=0)
    tiles_per = (counts + tm - 1) // tm
    tile_end = jnp.cumsum(tiles_per)
    tile_start = tile_end - tiles_per
    slot_of = (tile_start[experts] * tm + rank).astype(I32)
    max_tiles = (TOP_K * n) // tm + nexp
    tokens = jnp.tile(jnp.arange(n, dtype=I32), TOP_K)
    src_token = jnp.zeros((max_tiles * tm,), I32).at[slot_of].set(tokens)
    n_valid = tile_end[-1:].astype(I32)
    tile_ids = jnp.arange(max_tiles, dtype=I32)
    tile_expert = jnp.minimum(
        jnp.sum((tile_ids[:, None] >= tile_end[None, :]).astype(I32), axis=1), nexp - 1)
    tile_rows = jnp.clip(counts[tile_expert] - (tile_ids - tile_start[tile_expert]) * tm,
                         0, tm).astype(I32)
    ys = _grouped_ffn(xp, wg, wu, wd, j, tile_expert, n_valid, tile_rows, src_token, tm=tm)
    return _combine(h, routing, ys, slot_of, final_g, tm=min(COMBINE_TM, n))


def _norm_kernel(x_ref, g_ref, out_ref):
    out_ref[...] = _rms(x_ref[...], g_ref[...])


def _final_norm(h, g):
    n, d = h.shape
    tm = min(NORM_TM, n)
    return pl.pallas_call(
        _norm_kernel,
        grid=(n // tm,),
        in_specs=[pl.BlockSpec((tm, d), lambda i: (i, 0)),
                  pl.BlockSpec((1, d), lambda i: (0, 0))],
        out_specs=pl.BlockSpec((tm, d), lambda i: (i, 0)),
        out_shape=jax.ShapeDtypeStruct((n, d), F32),
        compiler_params=_cparams(1),
        name="final_norm",
    )(h, g.reshape(1, d))


def kernel(x, norm1_g, norm2_g, final_g, w_in, mlstm_gate_b, conv_w, mlstm_norm_g, pool_w, pool_scale, w_branch_m, w_branch_p, w_out, ffn_w_gate, ffn_w_up, ffn_w_down, router_w, router_b, exp_w_gate, exp_w_up, exp_w_down):
    batch, seq, d = x.shape
    depth = w_in.shape[0]
    nheads = mlstm_gate_b.shape[1] // 2
    dm = conv_w.shape[2] // 2
    dh = dm // nheads
    dp = pool_scale.shape[1]
    col_o_end = 4 * dm
    col_p = col_o_end + 2 * nheads
    col_g = col_p + dp
    assert col_o_end % d == 0 and (col_o_end + 2 * d) % dp == 0
    w_t = jnp.swapaxes(w_in, 1, 2)
    w_if = jnp.zeros((depth, LANES, d), F32).at[:, :2 * nheads].set(
        w_t[:, col_o_end:col_p]).astype(BF16)
    gate_block = col_o_end // d
    pool_block = (col_o_end + 2 * d) // dp
    wbm = w_branch_m.astype(BF16)
    wbp = w_branch_p.astype(BF16)
    wo = w_out.astype(BF16)
    pw = pool_w.astype(BF16)

    h = x.reshape(batch * seq, d)
    for l in range(depth):
        z, gates = _inproj(h, norm1_g[l], w_t, w_if, conv_w[l], l, seq=seq, dm=dm, dh=dh,
                           col_p=col_p, col_g=col_g, dp=dp)
        hm = _mlstm(z, gates, mlstm_gate_b[l], mlstm_norm_g[l],
                    batch=batch, seq=seq, nheads=nheads, dh=dh)
        j = l // 2
        router = None if l % 2 == 0 else (norm2_g[l], router_w[j], router_b[j])
        merged = _merge(h, hm, z, pw[l], pool_scale[l], wbm, wbp, wo, l, router,
                        seq=seq, gate_block=gate_block, pool_block=pool_block)
        if l % 2 == 0:
            h = _ffn(merged[0], norm2_g[l], ffn_w_gate, ffn_w_up, ffn_w_down, j)
        else:
            fg = final_g if l == depth - 1 else None
            h, routing, xp = merged
            h = _moe(h, routing, xp, router_w.shape[2], exp_w_gate, exp_w_up, exp_w_down, j, fg)
    if depth % 2 == 1:
        h = _final_norm(h, final_g)
    return h.reshape(batch, seq, d)
```

```python
import functools

import jax
import jax.numpy as jnp
from jax import lax
from jax.experimental import pallas as pl
from jax.experimental.pallas import tpu as pltpu

F32 = jnp.float32
BF16 = jnp.bfloat16
I32 = jnp.int32

EPS = 1e-6
CONV_K = 4
POOL_WINDOWS = (2, 4, 8, 16)
POOL_HALO = 16
TOP_K = 2
LANES = 128
SUBLANES = 8
NEG_BIG = -1e30
HI16 = -65536
VMEM_LIMIT = 56 * 1024 * 1024
VMEM_LIMIT_MERGE = 59 * 1024 * 1024

INPROJ_TM = 1024
INPROJ_TN = 1024
INPROJ_SUB = 256
MLSTM_T = 1024
MLSTM_CHUNK = 256
MERGE_TM = 512
MERGE_SUB = 512
FFN_TM = 1024
FFN_TF = 512
FFN_SUB = 256
MOE_TM = 1024
MOE_SUB = 256
COMBINE_TM = 256
NORM_TM = 1024
DMA_UNROLL = 8


def _cparams(n_axes, vmem_limit=VMEM_LIMIT):
    return pltpu.CompilerParams(
        dimension_semantics=("arbitrary",) * n_axes,
        vmem_limit_bytes=vmem_limit,
    )


def _rms(x, g):
    ms = jnp.mean(x * x, axis=-1, keepdims=True)
    return x * lax.rsqrt(ms + EPS) * g


def _sigmoid(x):
    return 1.0 / (1.0 + jnp.exp(-x))


def _inproj_kernel(x_ref, g_ref, w_ref, wif_ref, cw_ref, z_ref, gates_ref, xn_ref, halo_ref,
                   *, tiles_per_seq, n_q, n_qk, n_qkvo, k_scale):
    i = pl.program_id(0)
    j = pl.program_id(1)
    tm = x_ref.shape[0]
    n_v_end = n_qk + (n_qk - n_q)

    @pl.when(jnp.logical_and(i == 0, j == 0))
    def _():
        halo_ref[...] = jnp.zeros_like(halo_ref)

    @pl.when(j == 0)
    def _():
        xn = _rms(x_ref[...], g_ref[...]).astype(BF16)
        xn_ref[...] = xn
        gates_ref[...] = lax.dot_general(xn, wif_ref[...], (((1,), (1,)), ((), ())),
                                         preferred_element_type=F32)

    tn = z_ref.shape[1]
    sub = min(INPROJ_SUB, tn)
    col_blocks = [slice(c, c + sub) for c in range(0, tn, sub)]

    def project(cs):
        return lax.dot_general(xn_ref[...], w_ref[0, cs, :].astype(BF16),
                               (((1,), (1,)), ((), ())), preferred_element_type=F32)

    @pl.when(j < n_qk)
    def _():
        first = i % tiles_per_seq == 0
        scale = jnp.where(j >= n_q, k_scale, 1.0)
        for cs in col_blocks:
            acc = project(cs)
            halo = jnp.where(first, 0.0, halo_ref[j, :, cs])
            ext = jnp.concatenate([halo, acc], axis=0)
            w = cw_ref[:, cs]
            y = w[CONV_K - 1:CONV_K] * acc
            for d in range(1, CONV_K):
                y = y + w[CONV_K - 1 - d:CONV_K - d] * pltpu.roll(ext, d, axis=0)[SUBLANES:]
            halo_ref[j, :, cs] = acc[tm - SUBLANES:]
            hy = y * (0.5 * scale)
            y = hy + hy * jnp.tanh(0.5 * y)
            z_ref[:, cs] = y.astype(BF16)

    @pl.when(jnp.logical_and(j >= n_v_end, j < n_qkvo))
    def _():
        for cs in col_blocks:
            z_ref[:, cs] = _sigmoid(project(cs)).astype(BF16)

    @pl.when(jnp.logical_or(jnp.logical_and(j >= n_qk, j < n_v_end), j >= n_qkvo))
    def _():
        z_ref[...] = project(slice(None)).astype(BF16)


def _inproj(h, g, w_t, w_if, conv_w, layer, *, seq, dm, dh, col_p, col_g, dp):
    n, d = h.shape
    tm = min(INPROJ_TM, n, seq)
    tn = min(INPROJ_TN, dm)
    cols = 4 * dm + 2 * d + dp
    assert seq % tm == 0 and dm % tn == 0 and (2 * d) % tn == 0 and dp % tn == 0
    assert col_p % SUBLANES == 0 and col_g % SUBLANES == 0
    n_q = dm // tn
    n_qkvo = 4 * n_q
    n_gate = (2 * d) // tn
    cw = jnp.zeros((SUBLANES, 2 * dm), F32).at[:CONV_K].set(conv_w)

    def src_row(j):
        row = jnp.where(j < n_qkvo, j * tn,
                        jnp.where(j < n_qkvo + n_gate, col_g + (j - n_qkvo) * tn,
                                  col_p + (j - n_qkvo - n_gate) * tn))
        return pl.multiple_of(row, SUBLANES)

    return pl.pallas_call(
        functools.partial(_inproj_kernel, tiles_per_seq=seq // tm, n_q=n_q, n_qk=2 * n_q,
                          n_qkvo=n_qkvo, k_scale=dh ** -0.5),
        grid=(n // tm, cols // tn),
        in_specs=[
            pl.BlockSpec((tm, d), lambda i, j: (i, 0)),
            pl.BlockSpec((1, d), lambda i, j: (0, 0)),
            pl.BlockSpec((pl.Element(1), pl.Element(tn), pl.Element(d)),
                         lambda i, j: (layer, src_row(j), 0)),
            pl.BlockSpec((None, LANES, d), lambda i, j: (layer, 0, 0)),
            pl.BlockSpec((SUBLANES, tn), lambda i, j: (0, jnp.minimum(j, 2 * n_q - 1))),
        ],
        out_specs=[
            pl.BlockSpec((tm, tn), lambda i, j: (i, j)),
            pl.BlockSpec((tm, LANES), lambda i, j: (i, 0)),
        ],
        out_shape=[
            jax.ShapeDtypeStruct((n, cols), BF16),
            jax.ShapeDtypeStruct((n, LANES), F32),
        ],
        scratch_shapes=[pltpu.VMEM((tm, d), BF16),
                        pltpu.VMEM((2 * n_q, SUBLANES, tn), F32)],
        compiler_params=_cparams(2),
        name="inproj",
    )(h, g.reshape(1, d), w_t, w_if, cw)


def _mlstm_kernel(q_ref, k_ref, v_ref, og_ref, gt_ref, gb_ref, ng_ref,
                  out_ref, ct_ref, m_ref, *, chunk, nheads):
    t_rows, dm = q_ref.shape
    dh = dm // nheads

    @pl.when(pl.program_id(1) == 0)
    def _():
        ct_ref[...] = jnp.zeros_like(ct_ref)
        m_ref[...] = jnp.zeros_like(m_ref)

    gts = gt_ref[...] + gb_ref[...]
    lf = jnp.minimum(gts, 0.0) - jnp.log(1.0 + jnp.exp(-jnp.abs(gts)))
    in_chunk = lax.broadcasted_iota(I32, gts.shape, 0) & (chunk - 1)
    b = lf
    d = 1
    while d < chunk:
        b = b + jnp.where(in_chunk >= d, pltpu.roll(b, d, axis=0), 0.0)
        d *= 2
    u = gts - pltpu.roll(b, LANES - nheads, axis=1)
    d = 1
    while d < chunk:
        u = jnp.maximum(u, jnp.where(in_chunk >= d, pltpu.roll(u, d, axis=0), NEG_BIG))
        d *= 2
    gts_t = gts.T
    b_t = b.T

    tt = lax.broadcasted_iota(I32, (chunk, chunk), 0)
    ss = lax.broadcasted_iota(I32, (chunk, chunk), 1)
    causal = tt >= ss

    def wide(x, reps):
        return jnp.concatenate([x] * reps, axis=1)

    reps = dh // LANES
    ones_blk = jnp.ones((chunk, LANES), BF16)
    mean_blk = jnp.full((dh, LANES), 1.0 / dh, BF16)
    state = [ct_ref[hd] for hd in range(nheads)]
    m_run = [m_ref[hd] for hd in range(nheads)]
    i_rep = [jnp.broadcast_to(gts[:, hd:hd + 1], gts.shape) for hd in range(nheads)]
    b_rep = [jnp.broadcast_to(b[:, nheads + hd:nheads + hd + 1], gts.shape)
             for hd in range(nheads)]
    u_rep = [jnp.broadcast_to(u[:, hd:hd + 1], gts.shape) for hd in range(nheads)]
    for c in range(t_rows // chunk):
        lo, hi = c * chunk, (c + 1) * chunk
        for hd in range(nheads):
            cs = slice(hd * dh, (hd + 1) * dh)
            qb, kb = q_ref[lo:hi, cs], k_ref[lo:hi, cs]
            v_ext = jnp.concatenate([v_ref[lo:hi, cs], ones_blk], axis=1)
            bc, ic, uc = b_rep[hd][lo:hi], i_rep[hd][lo:hi], u_rep[hd][lo:hi]
            br, ir = b_t[nheads + hd:nheads + hd + 1, lo:hi], gts_t[hd:hd + 1, lo:hi]
            m_prev = m_run[hd]
            inter = bc + m_prev
            mt = jnp.maximum(inter, bc + uc)
            wi = jnp.exp(inter - mt)
            dmat = jnp.where(causal, wide(bc, chunk // LANES) - br + ir, NEG_BIG)
            s = lax.dot_general(qb, kb, (((1,), (1,)), ((), ())),
                                preferred_element_type=F32)
            s = s * jnp.exp(dmat - wide(mt, chunk // LANES))
            nd = wide(wi, reps + 1) * jnp.dot(qb, state[hd].astype(BF16),
                                              preferred_element_type=F32)
            nd = nd + jnp.dot(s.astype(BF16), v_ext, preferred_element_type=F32)
            num, den = nd[:, :dh], nd[:, dh:]
            rden = 1.0 / jnp.maximum(jnp.abs(den), jnp.exp(-mt))
            ms = jnp.dot((num * num).astype(BF16), mean_blk, preferred_element_type=F32)
            scale = rden * lax.rsqrt(ms * (rden * rden) + EPS)
            hn = num * wide(scale, reps) * ng_ref[:, cs]
            out_ref[lo:hi, cs] = (og_ref[lo:hi, cs].astype(F32) * hn).astype(BF16)
            bl = bc[chunk - 1:chunk]
            gcol = bl - bc + ic
            m_new = jnp.maximum(bl + m_prev, jnp.max(gcol, axis=0, keepdims=True))
            a = jnp.exp(bl + m_prev - m_new)
            kw = kb.astype(F32) * wide(jnp.exp(gcol - m_new), reps)
            state[hd] = wide(a, reps + 1) * state[hd] + lax.dot_general(
                kw.astype(BF16), v_ext, (((0,), (0,)), ((), ())), preferred_element_type=F32)
            m_run[hd] = m_new
    for hd in range(nheads):
        ct_ref[hd] = state[hd]
        m_ref[hd] = m_run[hd]


def _mlstm(z, gates, gate_b, norm_g, *, batch, seq, nheads, dh):
    n = z.shape[0]
    t = min(MLSTM_T, seq)
    chunk = min(MLSTM_CHUNK, t)
    spb = seq // t
    dm = nheads * dh
    gb = jnp.zeros((1, LANES), F32).at[0, :2 * nheads].set(gate_b)

    def zspec(col_block):
        return pl.BlockSpec((t, dm), lambda b, s: (b * spb + s, col_block))

    return pl.pallas_call(
        functools.partial(_mlstm_kernel, chunk=chunk, nheads=nheads),
        grid=(batch, spb),
        in_specs=[
            zspec(0), zspec(1), zspec(2), zspec(3),
            pl.BlockSpec((t, LANES), lambda b, s: (b * spb + s, 0)),
            pl.BlockSpec((1, LANES), lambda b, s: (0, 0)),
            pl.BlockSpec((1, dm), lambda b, s: (0, 0)),
        ],
        out_specs=pl.BlockSpec((t, dm), lambda b, s: (b * spb + s, 0)),
        out_shape=jax.ShapeDtypeStruct((n, dm), BF16),
        scratch_shapes=[
            pltpu.VMEM((nheads, dh, dh + LANES), F32),
            pltpu.VMEM((nheads, 1, LANES), F32),
        ],
        compiler_params=_cparams(2),
        name="mlstm",
    )(z, z, z, z, gates, gb, norm_g.reshape(1, dm))


def _pool_mix(p_ref, halo_ref, w_ref, sc_ref, pos0):
    t_rows = p_ref.shape[0]
    grp = w_ref.shape[1]
    p = p_ref[...].astype(F32)
    halo = jnp.where(pos0 == 0, 0.0, halo_ref[...].astype(F32))
    ext = jnp.concatenate([halo, p], axis=0)
    pos = pos0 + lax.broadcasted_iota(I32, (t_rows, 1), 0)
    posf = (pos + 1).astype(F32)
    outs = []
    for j, w in enumerate(POOL_WINDOWS):
        cs = slice(j * grp, (j + 1) * grp)
        win = ext[:, cs]
        d = 1
        while d < w:
            win = win + pltpu.roll(win, d, axis=0)
            d *= 2
        pooled = win[POOL_HALO:] / jnp.minimum(posf, float(w)) - p[:, cs]
        y = jnp.dot(pooled.astype(BF16), w_ref[j], preferred_element_type=F32)
        outs.append((y * sc_ref[:, cs]).astype(BF16))
    return jnp.concatenate(outs, axis=1)


def _route(xn, wh, wl, rb, nexp):
    xh = xn.astype(BF16)
    xhf = xh.astype(F32)
    xl = (xn - xhf).astype(BF16)
    half = xn.shape[1] // 2
    lo = lax.shift_right_logical(lax.bitcast_convert_type(xhf[:, :half], I32), 16)
    hi = lax.bitcast_convert_type(xhf[:, half:], I32) & HI16
    logits = (jnp.dot(xh, wh, preferred_element_type=F32)
              + jnp.dot(xh, wl, preferred_element_type=F32)
              + jnp.dot(xl, wh, preferred_element_type=F32)) + rb
    lane = lax.broadcasted_iota(I32, logits.shape, 1)
    logits = jnp.where(lane < nexp, logits, NEG_BIG)
    v1 = jnp.max(logits, axis=1, keepdims=True)
    i1 = jnp.min(jnp.where(logits == v1, lane, LANES), axis=1, keepdims=True)
    rest = jnp.where(lane == i1, NEG_BIG, logits)
    v2 = jnp.max(rest, axis=1, keepdims=True)
    i2 = jnp.min(jnp.where(rest == v2, lane, LANES), axis=1, keepdims=True)
    e2 = jnp.exp(v2 - v1)
    g1 = 1.0 / (1.0 + e2)
    g2 = e2 / (1.0 + e2)
    out = jnp.where(lane == 0, i1.astype(F32), 0.0)
    out = jnp.where(lane == 1, i2.astype(F32), out)
    out = jnp.where(lane == 2, g1, out)
    out = jnp.where(lane == 3, g2, out)
    return out, hi | lo


def _merge_kernel(*refs, tiles_per_seq, nexp):
    (h_ref, hm_ref, p_ref, halo_ref, gm_ref, gp_ref, pw_ref, psc_ref,
     wbm_ref, wbp_ref, wo_ref) = refs[:11]
    tm = h_ref.shape[0]
    pos0 = (pl.program_id(0) % tiles_per_seq) * tm
    hp = _pool_mix(p_ref, halo_ref, pw_ref, psc_ref, pos0)
    hm = hm_ref[...]
    d = h_ref.shape[1]
    sub = min(MERGE_SUB, d)
    out = h_ref[...]
    for c in range(0, d, sub):
        cs = slice(c, c + sub)
        ym = jnp.dot(hm, wbm_ref[:, cs], preferred_element_type=F32)
        yp = jnp.dot(hp, wbp_ref[:, cs], preferred_element_type=F32)
        y = (_sigmoid(gm_ref[:, cs].astype(F32)) * ym
             + _sigmoid(gp_ref[:, cs].astype(F32)) * yp)
        out = out + jnp.dot(y.astype(BF16), wo_ref[cs, :], preferred_element_type=F32)
    if nexp is None:
        refs[11][...] = out
    else:
        n2g_ref, rwh_ref, rwl_ref, rb_ref, out_ref, rt_ref, xp_ref = refs[11:]
        out_ref[...] = out
        rt_ref[...], xp_ref[...] = _route(_rms(out, n2g_ref[...]), rwh_ref[...],
                                          rwl_ref[...], rb_ref[...], nexp)


def _merge(h, hm, z, pool_w, pool_scale, wbm, wbp, wo, layer, router, *,
           seq, gate_block, pool_block):
    n, d = h.shape
    dm = hm.shape[1]
    ngrp, grp, _ = pool_w.shape
    dp = ngrp * grp
    tm = min(MERGE_TM, n, seq)
    assert seq % tm == 0 and tm % POOL_HALO == 0
    hb = tm // POOL_HALO
    once = pl.Buffered(1)
    in_specs = [
        pl.BlockSpec((tm, d), lambda i: (i, 0)),
        pl.BlockSpec((tm, dm), lambda i: (i, 0)),
        pl.BlockSpec((tm, dp), lambda i: (i, pool_block)),
        pl.BlockSpec((POOL_HALO, dp), lambda i: (jnp.maximum(i * hb - 1, 0), pool_block)),
        pl.BlockSpec((tm, d), lambda i: (i, gate_block)),
        pl.BlockSpec((tm, d), lambda i: (i, gate_block + 1)),
        pl.BlockSpec((ngrp, grp, grp), lambda i: (0, 0, 0)),
        pl.BlockSpec((1, dp), lambda i: (0, 0)),
        pl.BlockSpec((None, dm, d), lambda i: (layer, 0, 0), pipeline_mode=once),
        pl.BlockSpec((None, dp, d), lambda i: (layer, 0, 0), pipeline_mode=once),
        pl.BlockSpec((None, d, d), lambda i: (layer, 0, 0), pipeline_mode=once),
    ]
    args = [h, hm, z, z, z, z, pool_w, pool_scale.reshape(1, dp), wbm, wbp, wo]
    out_specs = [pl.BlockSpec((tm, d), lambda i: (i, 0))]
    out_shape = [jax.ShapeDtypeStruct((n, d), F32)]
    nexp = None
    if router is not None:
        g2, router_w, router_b = router
        nexp = router_w.shape[1]
        w = jnp.zeros((d, LANES), F32).at[:, :nexp].set(router_w)
        wh = w.astype(BF16)
        wl = (w - wh.astype(F32)).astype(BF16)
        rb = jnp.zeros((1, LANES), F32).at[0, :nexp].set(router_b)
        in_specs += [
            pl.BlockSpec((1, d), lambda i: (0, 0)),
            pl.BlockSpec((d, LANES), lambda i: (0, 0)),
            pl.BlockSpec((d, LANES), lambda i: (0, 0)),
            pl.BlockSpec((1, LANES), lambda i: (0, 0)),
        ]
        args += [g2.reshape(1, d), wh, wl, rb]
        out_specs += [pl.BlockSpec((tm, LANES), lambda i: (i, 0)),
                      pl.BlockSpec((tm, d // 2), lambda i: (i, 0))]
        out_shape += [jax.ShapeDtypeStruct((n, LANES), F32),
                      jax.ShapeDtypeStruct((n, d // 2), I32)]
    return pl.pallas_call(
        functools.partial(_merge_kernel, tiles_per_seq=seq // tm, nexp=nexp),
        grid=(n // tm,),
        in_specs=in_specs,
        out_specs=out_specs,
        out_shape=out_shape,
        compiler_params=_cparams(1, VMEM_LIMIT_MERGE),
        name="merge",
    )(*args)


def _swiglu_step(xn_ref, wg_ref, wu_ref, wd_ref, out_ref, rows=slice(None)):
    xn = xn_ref[rows, :]
    tf = wg_ref.shape[1]
    sub = min(FFN_SUB, tf)
    for c in range(0, tf, sub):
        cs = slice(c, c + sub)
        a = jnp.dot(xn, wg_ref[:, cs].astype(BF16), preferred_element_type=F32)
        u = jnp.dot(xn, wu_ref[:, cs].astype(BF16), preferred_element_type=F32)
        act = (a * _sigmoid(a) * u).astype(BF16)
        out_ref[rows, :] += jnp.dot(act, wd_ref[cs, :].astype(BF16),
                                    preferred_element_type=F32)


def _ffn_kernel(x_hbm, g_ref, wg_ref, wu_ref, wd_ref, out_ref, xn_ref, sem):
    tm = out_ref.shape[0]

    @pl.when(pl.program_id(1) == 0)
    def _():
        rows = pl.ds(pl.multiple_of(pl.program_id(0) * tm, tm), tm)
        copy = pltpu.make_async_copy(x_hbm.at[rows], out_ref, sem)
        copy.start()
        copy.wait()
        xn_ref[...] = _rms(out_ref[...], g_ref[...]).astype(BF16)

    _swiglu_step(xn_ref, wg_ref, wu_ref, wd_ref, out_ref)


def _ffn(h, g, wg, wu, wd, j):
    n, d = h.shape
    ff = wg.shape[2]
    tm = min(FFN_TM, n)
    tf = min(FFN_TF, ff)
    return pl.pallas_call(
        _ffn_kernel,
        grid=(n // tm, ff // tf),
        in_specs=[
            pl.BlockSpec(memory_space=pl.ANY),
            pl.BlockSpec((1, d), lambda i, f: (0, 0)),
            pl.BlockSpec((None, d, tf), lambda i, f: (j, 0, f)),
            pl.BlockSpec((None, d, tf), lambda i, f: (j, 0, f)),
            pl.BlockSpec((None, tf, d), lambda i, f: (j, f, 0)),
        ],
        out_specs=pl.BlockSpec((tm, d), lambda i, f: (i, 0)),
        out_shape=jax.ShapeDtypeStruct((n, d), F32),
        scratch_shapes=[pltpu.VMEM((tm, d), BF16), pltpu.SemaphoreType.DMA(())],
        compiler_params=_cparams(2),
        name="ffn",
    )(h, g.reshape(1, d), wg, wu, wd)


def _gffn_kernel(te_ref, nv_ref, cnt_ref, src_ref, xp_hbm, wg_ref, wu_ref, wd_ref, out_ref,
                 xg_ref, xn_ref, sem, *, tm, sub, issue_steps):
    t = pl.program_id(0)
    f = pl.program_id(1)
    nf = pl.num_programs(1)
    nv = nv_ref[0]
    cnt = cnt_ref[t]
    half = xg_ref.shape[2]
    rows_per_step = tm // issue_steps
    weights = (wg_ref, wu_ref, wd_ref)

    def slot_copy(slot):
        return pltpu.make_async_copy(xp_hbm.at[pl.ds(0, tm)], xg_ref.at[slot], sem.at[slot])

    def row_copy(tile, slot, r):
        tok = src_ref[tile * tm + r]
        return pltpu.make_async_copy(xp_hbm.at[pl.ds(tok, 1)],
                                     xg_ref.at[slot, pl.ds(r, 1)], sem.at[slot])

    def prefetch_rows():
        nxt = jnp.minimum(t + 1, nv - 1)
        slot = (t + 1) % 2
        r0 = f * rows_per_step
        for r in range(rows_per_step):
            row_copy(nxt, slot, r0 + r).start()

    @pl.when(t < nv)
    def _():
        @pl.when(f == 0)
        def _():
            @pl.when(t == 0)
            def _():
                def body(r, carry):
                    row_copy(0, 0, r).start()
                    return carry

                lax.fori_loop(0, tm, body, 0, unroll=DMA_UNROLL)

            slot_copy(t % 2).wait()
            p = xg_ref[t % 2]
            xn_ref[:, :half] = lax.bitcast_convert_type(lax.shift_left(p, 16), F32).astype(BF16)
            xn_ref[:, half:] = lax.bitcast_convert_type(p & HI16, F32).astype(BF16)
            out_ref[...] = jnp.zeros_like(out_ref)

        full = cnt == tm
        issuing = f < issue_steps

        @pl.when(jnp.logical_and(full, issuing))
        def _():
            prefetch_rows()
            _swiglu_step(xn_ref, *weights, out_ref)

        @pl.when(jnp.logical_and(full, jnp.logical_not(issuing)))
        def _():
            _swiglu_step(xn_ref, *weights, out_ref)

        @pl.when(jnp.logical_not(full))
        def _():
            @pl.when(issuing)
            def _():
                prefetch_rows()

            for q in range(tm // sub):
                @pl.when(q * sub < cnt)
                def _():
                    _swiglu_step(xn_ref, *weights, out_ref, rows=slice(q * sub, (q + 1) * sub))

        @pl.when(jnp.logical_and(t == nv - 1, f == nf - 1))
        def _():
            slot_copy((t + 1) % 2).wait()

    @pl.when(jnp.logical_and(t >= nv, f == 0))
    def _():
        out_ref[...] = jnp.zeros_like(out_ref)


def _grouped_ffn(xp, wg, wu, wd, j, tile_expert, n_valid, tile_rows, src_token, *, tm):
    half = xp.shape[1]
    d = 2 * half
    slots = src_token.shape[0]
    ff = wg.shape[3]
    tf = min(FFN_TF, ff)
    nf = ff // tf
    ntiles = slots // tm
    sub = min(MOE_SUB, tm)
    issue_steps = 1
    while issue_steps * 2 <= nf and tm % (issue_steps * 2) == 0:
        issue_steps *= 2

    def fcol(t, f, nv):
        return jnp.where(t < nv[0], f, nf - 1)

    grid_spec = pltpu.PrefetchScalarGridSpec(
        num_scalar_prefetch=4,
        grid=(ntiles, nf),
        in_specs=[
            pl.BlockSpec(memory_space=pl.ANY),
            pl.BlockSpec((None, None, d, tf),
                         lambda t, f, te, nv, cnt, src: (j, te[t], 0, fcol(t, f, nv))),
            pl.BlockSpec((None, None, d, tf),
                         lambda t, f, te, nv, cnt, src: (j, te[t], 0, fcol(t, f, nv))),
            pl.BlockSpec((None, None, tf, d),
                         lambda t, f, te, nv, cnt, src: (j, te[t], fcol(t, f, nv), 0)),
        ],
        out_specs=pl.BlockSpec((tm, d), lambda t, f, te, nv, cnt, src: (t, 0),
                               pipeline_mode=pl.Buffered(1)),
        scratch_shapes=[
            pltpu.VMEM((2, tm, half), I32),
            pltpu.VMEM((tm, d), BF16),
            pltpu.SemaphoreType.DMA((2,)),
        ],
    )
    return pl.pallas_call(
        functools.partial(_gffn_kernel, tm=tm, sub=sub, issue_steps=issue_steps),
        grid_spec=grid_spec,
        out_shape=jax.ShapeDtypeStruct((slots, d), F32),
        compiler_params=_cparams(2),
        name="grouped_ffn",
    )(tile_expert, n_valid, tile_rows, src_token, xp, wg, wu, wd)


def _combine_kernel(slot_ref, h_ref, rt_ref, g_ref, ys_hbm, out_ref, buf_ref, sem,
                    *, tm, ntok, final_norm):
    i = pl.program_id(0)
    nt = pl.num_programs(0)

    def row_copy(tile, slot, kk, r):
        sl = slot_ref[kk * ntok + tile * tm + r]
        return pltpu.make_async_copy(ys_hbm.at[pl.ds(sl, 1)],
                                     buf_ref.at[slot, kk, pl.ds(r, 1)], sem.at[slot])

    def issue(tile, slot):
        for r in range(tm):
            for kk in range(TOP_K):
                row_copy(tile, slot, kk, r).start()

    @pl.when(i == 0)
    def _():
        issue(0, 0)

    for parity in range(2):
        @pl.when(jnp.logical_and(i + 1 < nt, (i + 1) % 2 == parity))
        def _(parity=parity):
            issue(i + 1, parity)

    slot = i % 2
    for kk in range(TOP_K):
        pltpu.make_async_copy(ys_hbm.at[pl.ds(0, tm)], buf_ref.at[slot, kk], sem.at[slot]).wait()
    rt = rt_ref[...]
    out = h_ref[...] + rt[:, 2:3] * buf_ref[slot, 0] + rt[:, 3:4] * buf_ref[slot, 1]
    if final_norm:
        out = _rms(out, g_ref[...])
    out_ref[...] = out


def _combine(h, routing, ys, slot_of, final_g, *, tm):
    n, d = h.shape
    g = jnp.ones((1, d), F32) if final_g is None else final_g.reshape(1, d)
    grid_spec = pltpu.PrefetchScalarGridSpec(
        num_scalar_prefetch=1,
        grid=(n // tm,),
        in_specs=[
            pl.BlockSpec((tm, d), lambda i, sl: (i, 0)),
            pl.BlockSpec((tm, LANES), lambda i, sl: (i, 0)),
            pl.BlockSpec((1, d), lambda i, sl: (0, 0)),
            pl.BlockSpec(memory_space=pl.ANY),
        ],
        out_specs=pl.BlockSpec((tm, d), lambda i, sl: (i, 0)),
        scratch_shapes=[pltpu.VMEM((2, TOP_K, tm, d), F32), pltpu.SemaphoreType.DMA((2,))],
    )
    return pl.pallas_call(
        functools.partial(_combine_kernel, tm=tm, ntok=n, final_norm=final_g is not None),
        grid_spec=grid_spec,
        out_shape=jax.ShapeDtypeStruct((n, d), F32),
        compiler_params=_cparams(1),
        name="combine",
    )(slot_of, h, routing, g, ys)


def _moe(h, routing, xp, nexp, wg, wu, wd, j, final_g):
    n, d = h.shape
    tm = min(MOE_TM, n)
    experts = routing[:, :TOP_K].astype(I32).T.reshape(-1)
    onehot = (experts[:, None] == jnp.arange(nexp, dtype=I32)[None, :]).astype(I32)
    rank = jnp.sum((jnp.cumsum(onehot, axis=0) - onehot) * onehot, axis=1)
    counts = jnp.sum(onehot, axis=0)
    tiles_per = (counts + tm - 1) // tm
    tile_end = jnp.cumsum(tiles_per)
    tile_start = tile_end - tiles_per
    slot_of = (tile_start[experts] * tm + rank).astype(I32)
    max_tiles = (TOP_K * n) // tm + nexp
    tokens = jnp.tile(jnp.arange(n, dtype=I32), TOP_K)
    src_token = jnp.zeros((max_tiles * tm,), I32).at[slot_of].set(tokens)
    n_valid = tile_end[-1:].astype(I32)
    tile_ids = jnp.arange(max_tiles, dtype=I32)
    tile_expert = jnp.minimum(
        jnp.sum((tile_ids[:, None] >= tile_end[None, :]).astype(I32), axis=1), nexp - 1)
    tile_rows = jnp.clip(counts[tile_expert] - (tile_ids - tile_start[tile_expert]) * tm,
                         0, tm).astype(I32)
    ys = _grouped_ffn(xp, wg, wu, wd, j, tile_expert, n_valid, tile_rows, src_token, tm=tm)
    return _combine(h, routing, ys, slot_of, final_g, tm=min(COMBINE_TM, n))


def _norm_kernel(x_ref, g_ref, out_ref):
    out_ref[...] = _rms(x_ref[...], g_ref[...])


def _final_norm(h, g):
    n, d = h.shape
    tm = min(NORM_TM, n)
    return pl.pallas_call(
        _norm_kernel,
        grid=(n // tm,),
        in_specs=[pl.BlockSpec((tm, d), lambda i: (i, 0)),
                  pl.BlockSpec((1, d), lambda i: (0, 0))],
        out_specs=pl.BlockSpec((tm, d), lambda i: (i, 0)),
        out_shape=jax.ShapeDtypeStruct((n, d), F32),
        compiler_params=_cparams(1),
        name="final_norm",
    )(h, g.reshape(1, d))


def kernel(x, norm1_g, norm2_g, final_g, w_in, mlstm_gate_b, conv_w, mlstm_norm_g, pool_w, pool_scale, w_branch_m, w_branch_p, w_out, ffn_w_gate, ffn_w_up, ffn_w_down, router_w, router_b, exp_w_gate, exp_w_up, exp_w_down):
    batch, seq, d = x.shape
    depth = w_in.shape[0]
    nheads = mlstm_gate_b.shape[1] // 2
    dm = conv_w.shape[2] // 2
    dh = dm // nheads
    dp = pool_scale.shape[1]
    col_o_end = 4 * dm
    col_p = col_o_end + 2 * nheads
    col_g = col_p + dp
    assert col_o_end % d == 0 and (col_o_end + 2 * d) % dp == 0
    w_t = jnp.swapaxes(w_in, 1, 2)
    w_if = jnp.zeros((depth, LANES, d), F32).at[:, :2 * nheads].set(
        w_t[:, col_o_end:col_p]).astype(BF16)
    gate_block = col_o_end // d
    pool_block = (col_o_end + 2 * d) // dp
    wbm = w_branch_m.astype(BF16)
    wbp = w_branch_p.astype(BF16)
    wo = w_out.astype(BF16)
    pw = pool_w.astype(BF16)

    h = x.reshape(batch * seq, d)
    for l in range(depth):
        z, gates = _inproj(h, norm1_g[l], w_t, w_if, conv_w[l], l, seq=seq, dm=dm, dh=dh,
                           col_p=col_p, col_g=col_g, dp=dp)
        hm = _mlstm(z, gates, mlstm_gate_b[l], mlstm_norm_g[l],
                    batch=batch, seq=seq, nheads=nheads, dh=dh)
        j = l // 2
        router = None if l % 2 == 0 else (norm2_g[l], router_w[j], router_b[j])
        merged = _merge(h, hm, z, pw[l], pool_scale[l], wbm, wbp, wo, l, router,
                        seq=seq, gate_block=gate_block, pool_block=pool_block)
        if l % 2 == 0:
            h = _ffn(merged[0], norm2_g[l], ffn_w_gate, ffn_w_up, ffn_w_down, j)
        else:
            fg = final_g if l == depth - 1 else None
            h, routing, xp = merged
            h = _moe(h, routing, xp, router_w.shape[2], exp_w_gate, exp_w_up, exp_w_down, j, fg)
    if depth % 2 == 1:
        h = _final_norm(h, final_g)
    return h.reshape(batch, seq, d)
```

```python
import functools

import jax
import jax.numpy as jnp
from jax import lax
from jax.experimental import pallas as pl
from jax.experimental.pallas import tpu as pltpu

F32 = jnp.float32
BF16 = jnp.bfloat16
I32 = jnp.int32

EPS = 1e-6
CONV_K = 4
POOL_WINDOWS = (2, 4, 8, 16)
POOL_HALO = 16
TOP_K = 2
LANES = 128
SUBLANES = 8
NEG_BIG = -1e30
HI16 = -65536
VMEM_LIMIT = 56 * 1024 * 1024
VMEM_LIMIT_MERGE = 59 * 1024 * 1024

INPROJ_TM = 1024
INPROJ_TN = 1024
INPROJ_SUB = 256
MLSTM_T = 1024
MLSTM_CHUNK = 256
MERGE_TM = 512
MERGE_SUB = 512
FFN_TM = 1024
FFN_TF = 512
FFN_SUB = 256
MOE_TM = 1024
MOE_SUB = 256
COMBINE_TM = 256
NORM_TM = 1024
DMA_UNROLL = 8


def _cparams(n_axes, vmem_limit=VMEM_LIMIT):
    return pltpu.CompilerParams(
        dimension_semantics=("arbitrary",) * n_axes,
        vmem_limit_bytes=vmem_limit,
    )


def _rms(x, g):
    ms = jnp.mean(x * x, axis=-1, keepdims=True)
    return x * lax.rsqrt(ms + EPS) * g


def _sigmoid(x):
    return 1.0 / (1.0 + jnp.exp(-x))


def _inproj_kernel(x_ref, g_ref, w_ref, wif_ref, cw_ref, z_ref, gates_ref, xn_ref, halo_ref,
                   *, tiles_per_seq, n_q, n_qk, n_qkvo, k_scale):
    i = pl.program_id(0)
    j = pl.program_id(1)
    tm = x_ref.shape[0]
    n_v_end = n_qk + (n_qk - n_q)

    @pl.when(jnp.logical_and(i == 0, j == 0))
    def _():
        halo_ref[...] = jnp.zeros_like(halo_ref)

    @pl.when(j == 0)
    def _():
        xn = _rms(x_ref[...], g_ref[...]).astype(BF16)
        xn_ref[...] = xn
        gates_ref[...] = lax.dot_general(xn, wif_ref[...], (((1,), (1,)), ((), ())),
                                         preferred_element_type=F32)

    tn = z_ref.shape[1]
    sub = min(INPROJ_SUB, tn)
    col_blocks = [slice(c, c + sub) for c in range(0, tn, sub)]

    def project(cs):
        return lax.dot_general(xn_ref[...], w_ref[0, cs, :].astype(BF16),
                               (((1,), (1,)), ((), ())), preferred_element_type=F32)

    @pl.when(j < n_qk)
    def _():
        first = i % tiles_per_seq == 0
        scale = jnp.where(j >= n_q, k_scale, 1.0)
        for cs in col_blocks:
            acc = project(cs)
            halo = jnp.where(first, 0.0, halo_ref[j, :, cs])
            ext = jnp.concatenate([halo, acc], axis=0)
            w = cw_ref[:, cs]
            y = w[CONV_K - 1:CONV_K] * acc
            for d in range(1, CONV_K):
                y = y + w[CONV_K - 1 - d:CONV_K - d] * pltpu.roll(ext, d, axis=0)[SUBLANES:]
            halo_ref[j, :, cs] = acc[tm - SUBLANES:]
            hy = y * (0.5 * scale)
            y = hy + hy * jnp.tanh(0.5 * y)
            z_ref[:, cs] = y.astype(BF16)

    @pl.when(jnp.logical_and(j >= n_v_end, j < n_qkvo))
    def _():
        for cs in col_blocks:
            z_ref[:, cs] = _sigmoid(project(cs)).astype(BF16)

    @pl.when(jnp.logical_or(jnp.logical_and(j >= n_qk, j < n_v_end), j >= n_qkvo))
    def _():
        z_ref[...] = project(slice(None)).astype(BF16)


def _inproj(h, g, w_t, w_if, conv_w, layer, *, seq, dm, dh, col_p, col_g, dp):
    n, d = h.shape
    tm = min(INPROJ_TM, n, seq)
    tn = min(INPROJ_TN, dm)
    cols = 4 * dm + 2 * d + dp
    assert seq % tm == 0 and dm % tn == 0 and (2 * d) % tn == 0 and dp % tn == 0
    assert col_p % SUBLANES == 0 and col_g % SUBLANES == 0
    n_q = dm // tn
    n_qkvo = 4 * n_q
    n_gate = (2 * d) // tn
    cw = jnp.zeros((SUBLANES, 2 * dm), F32).at[:CONV_K].set(conv_w)

    def src_row(j):
        row = jnp.where(j < n_qkvo, j * tn,
                        jnp.where(j < n_qkvo + n_gate, col_g + (j - n_qkvo) * tn,
                                  col_p + (j - n_qkvo - n_gate) * tn))
        return pl.multiple_of(row, SUBLANES)

    return pl.pallas_call(
        functools.partial(_inproj_kernel, tiles_per_seq=seq // tm, n_q=n_q, n_qk=2 * n_q,
                          n_qkvo=n_qkvo, k_scale=dh ** -0.5),
        grid=(n // tm, cols // tn),
        in_specs=[
            pl.BlockSpec((tm, d), lambda i, j: (i, 0)),
            pl.BlockSpec((1, d), lambda i, j: (0, 0)),
            pl.BlockSpec((pl.Element(1), pl.Element(tn), pl.Element(d)),
                         lambda i, j: (layer, src_row(j), 0)),
            pl.BlockSpec((None, LANES, d), lambda i, j: (layer, 0, 0)),
            pl.BlockSpec((SUBLANES, tn), lambda i, j: (0, jnp.minimum(j, 2 * n_q - 1))),
        ],
        out_specs=[
            pl.BlockSpec((tm, tn), lambda i, j: (i, j)),
            pl.BlockSpec((tm, LANES), lambda i, j: (i, 0)),
        ],
        out_shape=[
            jax.ShapeDtypeStruct((n, cols), BF16),
            jax.ShapeDtypeStruct((n, LANES), F32),
        ],
        scratch_shapes=[pltpu.VMEM((tm, d), BF16),
                        pltpu.VMEM((2 * n_q, SUBLANES, tn), F32)],
        compiler_params=_cparams(2),
        name="inproj",
    )(h, g.reshape(1, d), w_t, w_if, cw)


def _mlstm_kernel(q_ref, k_ref, v_ref, og_ref, gt_ref, gb_ref, ng_ref,
                  out_ref, ct_ref, m_ref, *, chunk, nheads):
    t_rows, dm = q_ref.shape
    dh = dm // nheads

    @pl.when(pl.program_id(1) == 0)
    def _():
        ct_ref[...] = jnp.zeros_like(ct_ref)
        m_ref[...] = jnp.zeros_like(m_ref)

    gts = gt_ref[...] + gb_ref[...]
    lf = jnp.minimum(gts, 0.0) - jnp.log(1.0 + jnp.exp(-jnp.abs(gts)))
    in_chunk = lax.broadcasted_iota(I32, gts.shape, 0) & (chunk - 1)
    b = lf
    d = 1
    while d < chunk:
        b = b + jnp.where(in_chunk >= d, pltpu.roll(b, d, axis=0), 0.0)
        d *= 2
    u = gts - pltpu.roll(b, LANES - nheads, axis=1)
    d = 1
    while d < chunk:
        u = jnp.maximum(u, jnp.where(in_chunk >= d, pltpu.roll(u, d, axis=0), NEG_BIG))
        d *= 2
    gts_t = gts.T
    b_t = b.T

    tt = lax.broadcasted_iota(I32, (chunk, chunk), 0)
    ss = lax.broadcasted_iota(I32, (chunk, chunk), 1)
    causal = tt >= ss

    def wide(x, reps):
        return jnp.concatenate([x] * reps, axis=1)

    reps = dh // LANES
    ones_blk = jnp.ones((chunk, LANES), BF16)
    mean_blk = jnp.full((dh, LANES), 1.0 / dh, BF16)
    state = [ct_ref[hd] for hd in range(nheads)]
    m_run = [m_ref[hd] for hd in range(nheads)]
    i_rep = [jnp.broadcast_to(gts[:, hd:hd + 1], gts.shape) for hd in range(nheads)]
    b_rep = [jnp.broadcast_to(b[:, nheads + hd:nheads + hd + 1], gts.shape)
             for hd in range(nheads)]
    u_rep = [jnp.broadcast_to(u[:, hd:hd + 1], gts.shape) for hd in range(nheads)]
    for c in range(t_rows // chunk):
        lo, hi = c * chunk, (c + 1) * chunk
        for hd in range(nheads):
            cs = slice(hd * dh, (hd + 1) * dh)
            qb, kb = q_ref[lo:hi, cs], k_ref[lo:hi, cs]
            v_ext = jnp.concatenate([v_ref[lo:hi, cs], ones_blk], axis=1)
            bc, ic, uc = b_rep[hd][lo:hi], i_rep[hd][lo:hi], u_rep[hd][lo:hi]
            br, ir = b_t[nheads + hd:nheads + hd + 1, lo:hi], gts_t[hd:hd + 1, lo:hi]
            m_prev = m_run[hd]
            inter = bc + m_prev
            mt = jnp.maximum(inter, bc + uc)
            wi = jnp.exp(inter - mt)
            dmat = jnp.where(causal, wide(bc, chunk // LANES) - br + ir, NEG_BIG)
            s = lax.dot_general(qb, kb, (((1,), (1,)), ((), ())),
                                preferred_element_type=F32)
            s = s * jnp.exp(dmat - wide(mt, chunk // LANES))
            nd = wide(wi, reps + 1) * jnp.dot(qb, state[hd].astype(BF16),
                                              preferred_element_type=F32)
            nd = nd + jnp.dot(s.astype(BF16), v_ext, preferred_element_type=F32)
            num, den = nd[:, :dh], nd[:, dh:]
            rden = 1.0 / jnp.maximum(jnp.abs(den), jnp.exp(-mt))
            ms = jnp.dot((num * num).astype(BF16), mean_blk, preferred_element_type=F32)
            scale = rden * lax.rsqrt(ms * (rden * rden) + EPS)
            hn = num * wide(scale, reps) * ng_ref[:, cs]
            out_ref[lo:hi, cs] = (og_ref[lo:hi, cs].astype(F32) * hn).astype(BF16)
            bl = bc[chunk - 1:chunk]
            gcol = bl - bc + ic
            m_new = jnp.maximum(bl + m_prev, jnp.max(gcol, axis=0, keepdims=True))
            a = jnp.exp(bl + m_prev - m_new)
            kw = kb.astype(F32) * wide(jnp.exp(gcol - m_new), reps)
            state[hd] = wide(a, reps + 1) * state[hd] + lax.dot_general(
                kw.astype(BF16), v_ext, (((0,), (0,)), ((), ())), preferred_element_type=F32)
            m_run[hd] = m_new
    for hd in range(nheads):
        ct_ref[hd] = state[hd]
        m_ref[hd] = m_run[hd]


def _mlstm(z, gates, gate_b, norm_g, *, batch, seq, nheads, dh):
    n = z.shape[0]
    t = min(MLSTM_T, seq)
    chunk = min(MLSTM_CHUNK, t)
    spb = seq // t
    dm = nheads * dh
    gb = jnp.zeros((1, LANES), F32).at[0, :2 * nheads].set(gate_b)

    def zspec(col_block):
        return pl.BlockSpec((t, dm), lambda b, s: (b * spb + s, col_block))

    return pl.pallas_call(
        functools.partial(_mlstm_kernel, chunk=chunk, nheads=nheads),
        grid=(batch, spb),
        in_specs=[
            zspec(0), zspec(1), zspec(2), zspec(3),
            pl.BlockSpec((t, LANES), lambda b, s: (b * spb + s, 0)),
            pl.BlockSpec((1, LANES), lambda b, s: (0, 0)),
            pl.BlockSpec((1, dm), lambda b, s: (0, 0)),
        ],
        out_specs=pl.BlockSpec((t, dm), lambda b, s: (b * spb + s, 0)),
        out_shape=jax.ShapeDtypeStruct((n, dm), BF16),
        scratch_shapes=[
            pltpu.VMEM((nheads, dh, dh + LANES), F32),
            pltpu.VMEM((nheads, 1, LANES), F32),
        ],
        compiler_params=_cparams(2),
        name="mlstm",
    )(z, z, z, z, gates, gb, norm_g.reshape(1, dm))


def _pool_mix(p_ref, halo_ref, w_ref, sc_ref, pos0):
    t_rows = p_ref.shape[0]
    grp = w_ref.shape[1]
    p = p_ref[...].astype(F32)
    halo = jnp.where(pos0 == 0, 0.0, halo_ref[...].astype(F32))
    ext = jnp.concatenate([halo, p], axis=0)
    pos = pos0 + lax.broadcasted_iota(I32, (t_rows, 1), 0)
    posf = (pos + 1).astype(F32)
    outs = []
    for j, w in enumerate(POOL_WINDOWS):
        cs = slice(j * grp, (j + 1) * grp)
        win = ext[:, cs]
        d = 1
        while d < w:
            win = win + pltpu.roll(win, d, axis=0)
            d *= 2
        pooled = win[POOL_HALO:] / jnp.minimum(posf, float(w)) - p[:, cs]
        y = jnp.dot(pooled.astype(BF16), w_ref[j], preferred_element_type=F32)
        outs.append((y * sc_ref[:, cs]).astype(BF16))
    return jnp.concatenate(outs, axis=1)


def _route(xn, wh, wl, rb, nexp):
    xh = xn.astype(BF16)
    xhf = xh.astype(F32)
    xl = (xn - xhf).astype(BF16)
    half = xn.shape[1] // 2
    lo = lax.shift_right_logical(lax.bitcast_convert_type(xhf[:, :half], I32), 16)
    hi = lax.bitcast_convert_type(xhf[:, half:], I32) & HI16
    logits = (jnp.dot(xh, wh, preferred_element_type=F32)
              + jnp.dot(xh, wl, preferred_element_type=F32)
              + jnp.dot(xl, wh, preferred_element_type=F32)) + rb
    lane = lax.broadcasted_iota(I32, logits.shape, 1)
    logits = jnp.where(lane < nexp, logits, NEG_BIG)
    v1 = jnp.max(logits, axis=1, keepdims=True)
    i1 = jnp.min(jnp.where(logits == v1, lane, LANES), axis=1, keepdims=True)
    rest = jnp.where(lane == i1, NEG_BIG, logits)
    v2 = jnp.max(rest, axis=1, keepdims=True)
    i2 = jnp.min(jnp.where(rest == v2, lane, LANES), axis=1, keepdims=True)
    e2 = jnp.exp(v2 - v1)
    g1 = 1.0 / (1.0 + e2)
    g2 = e2 / (1.0 + e2)
    out = jnp.where(lane == 0, i1.astype(F32), 0.0)
    out = jnp.where(lane == 1, i2.astype(F32), out)
    out = jnp.where(lane == 2, g1, out)
    out = jnp.where(lane == 3, g2, out)
    return out, hi | lo


def _merge_kernel(*refs, tiles_per_seq, nexp):
    (h_ref, hm_ref, p_ref, halo_ref, gm_ref, gp_ref, pw_ref, psc_ref,
     wbm_ref, wbp_ref, wo_ref) = refs[:11]
    tm = h_ref.shape[0]
    pos0 = (pl.program_id(0) % tiles_per_seq) * tm
    hp = _pool_mix(p_ref, halo_ref, pw_ref, psc_ref, pos0)
    hm = hm_ref[...]
    d = h_ref.shape[1]
    sub = min(MERGE_SUB, d)
    out = h_ref[...]
    for c in range(0, d, sub):
        cs = slice(c, c + sub)
        ym = jnp.dot(hm, wbm_ref[:, cs], preferred_element_type=F32)
        yp = jnp.dot(hp, wbp_ref[:, cs], preferred_element_type=F32)
        y = (_sigmoid(gm_ref[:, cs].astype(F32)) * ym
             + _sigmoid(gp_ref[:, cs].astype(F32)) * yp)
        out = out + jnp.dot(y.astype(BF16), wo_ref[cs, :], preferred_element_type=F32)
    if nexp is None:
        refs[11][...] = out
    else:
        n2g_ref, rwh_ref, rwl_ref, rb_ref, out_ref, rt_ref, xp_ref = refs[11:]
        out_ref[...] = out
        rt_ref[...], xp_ref[...] = _route(_rms(out, n2g_ref[...]), rwh_ref[...],
                                          rwl_ref[...], rb_ref[...], nexp)


def _merge(h, hm, z, pool_w, pool_scale, wbm, wbp, wo, layer, router, *,
           seq, gate_block, pool_block):
    n, d = h.shape
    dm = hm.shape[1]
    ngrp, grp, _ = pool_w.shape
    dp = ngrp * grp
    tm = min(MERGE_TM, n, seq)
    assert seq % tm == 0 and tm % POOL_HALO == 0
    hb = tm // POOL_HALO
    once = pl.Buffered(1)
    in_specs = [
        pl.BlockSpec((tm, d), lambda i: (i, 0)),
        pl.BlockSpec((tm, dm), lambda i: (i, 0)),
        pl.BlockSpec((tm, dp), lambda i: (i, pool_block)),
        pl.BlockSpec((POOL_HALO, dp), lambda i: (jnp.maximum(i * hb - 1, 0), pool_block)),
        pl.BlockSpec((tm, d), lambda i: (i, gate_block)),
        pl.BlockSpec((tm, d), lambda i: (i, gate_block + 1)),
        pl.BlockSpec((ngrp, grp, grp), lambda i: (0, 0, 0)),
        pl.BlockSpec((1, dp), lambda i: (0, 0)),
        pl.BlockSpec((None, dm, d), lambda i: (layer, 0, 0), pipeline_mode=once),
        pl.BlockSpec((None, dp, d), lambda i: (layer, 0, 0), pipeline_mode=once),
        pl.BlockSpec((None, d, d), lambda i: (layer, 0, 0), pipeline_mode=once),
    ]
    args = [h, hm, z, z, z, z, pool_w, pool_scale.reshape(1, dp), wbm, wbp, wo]
    out_specs = [pl.BlockSpec((tm, d), lambda i: (i, 0))]
    out_shape = [jax.ShapeDtypeStruct((n, d), F32)]
    nexp = None
    if router is not None:
        g2, router_w, router_b = router
        nexp = router_w.shape[1]
        w = jnp.zeros((d, LANES), F32).at[:, :nexp].set(router_w)
        wh = w.astype(BF16)
        wl = (w - wh.astype(F32)).astype(BF16)
        rb = jnp.zeros((1, LANES), F32).at[0, :nexp].set(router_b)
        in_specs += [
            pl.BlockSpec((1, d), lambda i: (0, 0)),
            pl.BlockSpec((d, LANES), lambda i: (0, 0)),
            pl.BlockSpec((d, LANES), lambda i: (0, 0)),
            pl.BlockSpec((1, LANES), lambda i: (0, 0)),
        ]
        args += [g2.reshape(1, d), wh, wl, rb]
        out_specs += [pl.BlockSpec((tm, LANES), lambda i: (i, 0)),
                      pl.BlockSpec((tm, d // 2), lambda i: (i, 0))]
        out_shape += [jax.ShapeDtypeStruct((n, LANES), F32),
                      jax.ShapeDtypeStruct((n, d // 2), I32)]
    return pl.pallas_call(
        functools.partial(_merge_kernel, tiles_per_seq=seq // tm, nexp=nexp),
        grid=(n // tm,),
        in_specs=in_specs,
        out_specs=out_specs,
        out_shape=out_shape,
        compiler_params=_cparams(1, VMEM_LIMIT_MERGE),
        name="merge",
    )(*args)


def _swiglu_step(xn_ref, wg_ref, wu_ref, wd_ref, out_ref, rows=slice(None)):
    xn = xn_ref[rows, :]
    tf = wg_ref.shape[1]
    sub = min(FFN_SUB, tf)
    for c in range(0, tf, sub):
        cs = slice(c, c + sub)
        a = jnp.dot(xn, wg_ref[:, cs].astype(BF16), preferred_element_type=F32)
        u = jnp.dot(xn, wu_ref[:, cs].astype(BF16), preferred_element_type=F32)
        act = (a * _sigmoid(a) * u).astype(BF16)
        out_ref[rows, :] += jnp.dot(act, wd_ref[cs, :].astype(BF16),
                                    preferred_element_type=F32)


def _ffn_kernel(x_hbm, g_ref, wg_ref, wu_ref, wd_ref, out_ref, xn_ref, sem):
    tm = out_ref.shape[0]

    @pl.when(pl.program_id(1) == 0)
    def _():
        rows = pl.ds(pl.multiple_of(pl.program_id(0) * tm, tm), tm)
        copy = pltpu.make_async_copy(x_hbm.at[rows], out_ref, sem)
        copy.start()
        copy.wait()
        xn_ref[...] = _rms(out_ref[...], g_ref[...]).astype(BF16)

    _swiglu_step(xn_ref, wg_ref, wu_ref, wd_ref, out_ref)


def _ffn(h, g, wg, wu, wd, j):
    n, d = h.shape
    ff = wg.shape[2]
    tm = min(FFN_TM, n)
    tf = min(FFN_TF, ff)
    return pl.pallas_call(
        _ffn_kernel,
        grid=(n // tm, ff // tf),
        in_specs=[
            pl.BlockSpec(memory_space=pl.ANY),
            pl.BlockSpec((1, d), lambda i, f: (0, 0)),
            pl.BlockSpec((None, d, tf), lambda i, f: (j, 0, f)),
            pl.BlockSpec((None, d, tf), lambda i, f: (j, 0, f)),
            pl.BlockSpec((None, tf, d), lambda i, f: (j, f, 0)),
        ],
        out_specs=pl.BlockSpec((tm, d), lambda i, f: (i, 0)),
        out_shape=jax.ShapeDtypeStruct((n, d), F32),
        scratch_shapes=[pltpu.VMEM((tm, d), BF16), pltpu.SemaphoreType.DMA(())],
        compiler_params=_cparams(2),
        name="ffn",
    )(h, g.reshape(1, d), wg, wu, wd)


def _gffn_kernel(te_ref, nv_ref, cnt_ref, src_ref, xp_hbm, wg_ref, wu_ref, wd_ref, out_ref,
                 xg_ref, xn_ref, sem, *, tm, sub, issue_steps):
    t = pl.program_id(0)
    f = pl.program_id(1)
    nf = pl.num_programs(1)
    nv = nv_ref[0]
    cnt = cnt_ref[t]
    half = xg_ref.shape[2]
    rows_per_step = tm // issue_steps
    weights = (wg_ref, wu_ref, wd_ref)

    def slot_copy(slot):
        return pltpu.make_async_copy(xp_hbm.at[pl.ds(0, tm)], xg_ref.at[slot], sem.at[slot])

    def row_copy(tile, slot, r):
        tok = src_ref[tile * tm + r]
        return pltpu.make_async_copy(xp_hbm.at[pl.ds(tok, 1)],
                                     xg_ref.at[slot, pl.ds(r, 1)], sem.at[slot])

    def prefetch_rows():
        nxt = jnp.minimum(t + 1, nv - 1)
        slot = (t + 1) % 2
        r0 = f * rows_per_step
        for r in range(rows_per_step):
            row_copy(nxt, slot, r0 + r).start()

    @pl.when(t < nv)
    def _():
        @pl.when(f == 0)
        def _():
            @pl.when(t == 0)
            def _():
                def body(r, carry):
                    row_copy(0, 0, r).start()
                    return carry

                lax.fori_loop(0, tm, body, 0, unroll=DMA_UNROLL)

            slot_copy(t % 2).wait()
            p = xg_ref[t % 2]
            xn_ref[:, :half] = lax.bitcast_convert_type(lax.shift_left(p, 16), F32).astype(BF16)
            xn_ref[:, half:] = lax.bitcast_convert_type(p & HI16, F32).astype(BF16)
            out_ref[...] = jnp.zeros_like(out_ref)

        full = cnt == tm
        issuing = f < issue_steps

        @pl.when(jnp.logical_and(full, issuing))
        def _():
            prefetch_rows()
            _swiglu_step(xn_ref, *weights, out_ref)

        @pl.when(jnp.logical_and(full, jnp.logical_not(issuing)))
        def _():
            _swiglu_step(xn_ref, *weights, out_ref)

        @pl.when(jnp.logical_not(full))
        def _():
            @pl.when(issuing)
            def _():
                prefetch_rows()

            for q in range(tm // sub):
                @pl.when(q * sub < cnt)
                def _():
                    _swiglu_step(xn_ref, *weights, out_ref, rows=slice(q * sub, (q + 1) * sub))

        @pl.when(jnp.logical_and(t == nv - 1, f == nf - 1))
        def _():
            slot_copy((t + 1) % 2).wait()

    @pl.when(jnp.logical_and(t >= nv, f == 0))
    def _():
        out_ref[...] = jnp.zeros_like(out_ref)


def _grouped_ffn(xp, wg, wu, wd, j, tile_expert, n_valid, tile_rows, src_token, *, tm):
    half = xp.shape[1]
    d = 2 * half
    slots = src_token.shape[0]
    ff = wg.shape[3]
    tf = min(FFN_TF, ff)
    nf = ff // tf
    ntiles = slots // tm
    sub = min(MOE_SUB, tm)
    issue_steps = 1
    while issue_steps * 2 <= nf and tm % (issue_steps * 2) == 0:
        issue_steps *= 2

    def fcol(t, f, nv):
        return jnp.where(t < nv[0], f, nf - 1)

    grid_spec = pltpu.PrefetchScalarGridSpec(
        num_scalar_prefetch=4,
        grid=(ntiles, nf),
        in_specs=[
            pl.BlockSpec(memory_space=pl.ANY),
            pl.BlockSpec((None, None, d, tf),
                         lambda t, f, te, nv, cnt, src: (j, te[t], 0, fcol(t, f, nv))),
            pl.BlockSpec((None, None, d, tf),
                         lambda t, f, te, nv, cnt, src: (j, te[t], 0, fcol(t, f, nv))),
            pl.BlockSpec((None, None, tf, d),
                         lambda t, f, te, nv, cnt, src: (j, te[t], fcol(t, f, nv), 0)),
        ],
        out_specs=pl.BlockSpec((tm, d), lambda t, f, te, nv, cnt, src: (t, 0),
                               pipeline_mode=pl.Buffered(1)),
        scratch_shapes=[
            pltpu.VMEM((2, tm, half), I32),
            pltpu.VMEM((tm, d), BF16),
            pltpu.SemaphoreType.DMA((2,)),
        ],
    )
    return pl.pallas_call(
        functools.partial(_gffn_kernel, tm=tm, sub=sub, issue_steps=issue_steps),
        grid_spec=grid_spec,
        out_shape=jax.ShapeDtypeStruct((slots, d), F32),
        compiler_params=_cparams(2),
        name="grouped_ffn",
    )(tile_expert, n_valid, tile_rows, src_token, xp, wg, wu, wd)


def _combine_kernel(slot_ref, h_ref, rt_ref, g_ref, ys_hbm, out_ref, buf_ref, sem,
                    *, tm, ntok, final_norm):
    i = pl.program_id(0)
    nt = pl.num_programs(0)

    def row_copy(tile, slot, kk, r):
        sl = slot_ref[kk * ntok + tile * tm + r]
        return pltpu.make_async_copy(ys_hbm.at[pl.ds(sl, 1)],
                                     buf_ref.at[slot, kk, pl.ds(r, 1)], sem.at[slot])

    def issue(tile, slot):
        for r in range(tm):
            for kk in range(TOP_K):
                row_copy(tile, slot, kk, r).start()

    @pl.when(i == 0)
    def _():
        issue(0, 0)

    for parity in range(2):
        @pl.when(jnp.logical_and(i + 1 < nt, (i + 1) % 2 == parity))
        def _(parity=parity):
            issue(i + 1, parity)

    slot = i % 2
    for kk in range(TOP_K):
        pltpu.make_async_copy(ys_hbm.at[pl.ds(0, tm)], buf_ref.at[slot, kk], sem.at[slot]).wait()
    rt = rt_ref[...]
    out = h_ref[...] + rt[:, 2:3] * buf_ref[slot, 0] + rt[:, 3:4] * buf_ref[slot, 1]
    if final_norm:
        out = _rms(out, g_ref[...])
    out_ref[...] = out


def _combine(h, routing, ys, slot_of, final_g, *, tm):
    n, d = h.shape
    g = jnp.ones((1, d), F32) if final_g is None else final_g.reshape(1, d)
    grid_spec = pltpu.PrefetchScalarGridSpec(
        num_scalar_prefetch=1,
        grid=(n // tm,),
        in_specs=[
            pl.BlockSpec((tm, d), lambda i, sl: (i, 0)),
            pl.BlockSpec((tm, LANES), lambda i, sl: (i, 0)),
            pl.BlockSpec((1, d), lambda i, sl: (0, 0)),
            pl.BlockSpec(memory_space=pl.ANY),
        ],
        out_specs=pl.BlockSpec((tm, d), lambda i, sl: (i, 0)),
        scratch_shapes=[pltpu.VMEM((2, TOP_K, tm, d), F32), pltpu.SemaphoreType.DMA((2,))],
    )
    return pl.pallas_call(
        functools.partial(_combine_kernel, tm=tm, ntok=n, final_norm=final_g is not None),
        grid_spec=grid_spec,
        out_shape=jax.ShapeDtypeStruct((n, d), F32),
        compiler_params=_cparams(1),
        name="combine",
    )(slot_of, h, routing, g, ys)


def _moe(h, routing, xp, nexp, wg, wu, wd, j, final_g):
    n, d = h.shape
    tm = min(MOE_TM, n)
    experts = routing[:, :TOP_K].astype(I32).T.reshape(-1)
    onehot = (experts[:, None] == jnp.arange(nexp, dtype=I32)[None, :]).astype(I32)
    rank = jnp.sum((jnp.cumsum(onehot, axis=0) - onehot) * onehot, axis=1)
    counts = jnp.sum(onehot, axis=0)
    tiles_per = (counts + tm - 1) // tm
    tile_end = jnp.cumsum(tiles_per)
    tile_start = tile_end - tiles_per
    slot_of = (tile_start[experts] * tm + rank).astype(I32)
    max_tiles = (TOP_K * n) // tm + nexp
    tokens = jnp.tile(jnp.arange(n, dtype=I32), TOP_K)
    n_valid = tile_end[-1:].astype(I32)
    tile_ids = jnp.arange(max_tiles, dtype=I32)
    tile_expert = jnp.minimum(
        jnp.sum((tile_ids[:, None] >= tile_end[None, :]).astype(I32), axis=1), nexp - 1)
    tile_rows = jnp.clip(counts[tile_expert] - (tile_ids - tile_start[tile_expert]) * tm,
                         0, tm).astype(I32)
    slots = max_tiles * tm
    slot_ids = jnp.arange(slots, dtype=I32)
    is_pad = (slot_ids % tm) >= jnp.repeat(tile_rows, tm)
    keys = jnp.concatenate([slot_of, jnp.where(is_pad, slot_ids, slot_ids + slots)])
    vals = jnp.concatenate([tokens, jnp.zeros((slots,), I32)])
    src_token = lax.sort_key_val(keys, vals)[1][:slots]
    ys = _grouped_ffn(xp, wg, wu, wd, j, tile_expert, n_valid, tile_rows, src_token, tm=tm)
    return _combine(h, routing, ys, slot_of, final_g, tm=min(COMBINE_TM, n))


def _norm_kernel(x_ref, g_ref, out_ref):
    out_ref[...] = _rms(x_ref[...], g_ref[...])


def _final_norm(h, g):
    n, d = h.shape
    tm = min(NORM_TM, n)
    return pl.pallas_call(
        _norm_kernel,
        grid=(n // tm,),
        in_specs=[pl.BlockSpec((tm, d), lambda i: (i, 0)),
                  pl.BlockSpec((1, d), lambda i: (0, 0))],
        out_specs=pl.BlockSpec((tm, d), lambda i: (i, 0)),
        out_shape=jax.ShapeDtypeStruct((n, d), F32),
        compiler_params=_cparams(1),
        name="final_norm",
    )(h, g.reshape(1, d))


def kernel(x, norm1_g, norm2_g, final_g, w_in, mlstm_gate_b, conv_w, mlstm_norm_g, pool_w, pool_scale, w_branch_m, w_branch_p, w_out, ffn_w_gate, ffn_w_up, ffn_w_down, router_w, router_b, exp_w_gate, exp_w_up, exp_w_down):
    batch, seq, d = x.shape
    depth = w_in.shape[0]
    nheads = mlstm_gate_b.shape[1] // 2
    dm = conv_w.shape[2] // 2
    dh = dm // nheads
    dp = pool_scale.shape[1]
    col_o_end = 4 * dm
    col_p = col_o_end + 2 * nheads
    col_g = col_p + dp
    assert col_o_end % d == 0 and (col_o_end + 2 * d) % dp == 0
    w_t = jnp.swapaxes(w_in, 1, 2)
    w_if = jnp.zeros((depth, LANES, d), F32).at[:, :2 * nheads].set(
        w_t[:, col_o_end:col_p]).astype(BF16)
    gate_block = col_o_end // d
    pool_block = (col_o_end + 2 * d) // dp
    wbm = w_branch_m.astype(BF16)
    wbp = w_branch_p.astype(BF16)
    wo = w_out.astype(BF16)
    pw = pool_w.astype(BF16)

    h = x.reshape(batch * seq, d)
    for l in range(depth):
        z, gates = _inproj(h, norm1_g[l], w_t, w_if, conv_w[l], l, seq=seq, dm=dm, dh=dh,
                           col_p=col_p, col_g=col_g, dp=dp)
        hm = _mlstm(z, gates, mlstm_gate_b[l], mlstm_norm_g[l],
                    batch=batch, seq=seq, nheads=nheads, dh=dh)
        j = l // 2
        router = None if l % 2 == 0 else (norm2_g[l], router_w[j], router_b[j])
        merged = _merge(h, hm, z, pw[l], pool_scale[l], wbm, wbp, wo, l, router,
                        seq=seq, gate_block=gate_block, pool_block=pool_block)
        if l % 2 == 0:
            h = _ffn(merged[0], norm2_g[l], ffn_w_gate, ffn_w_up, ffn_w_down, j)
        else:
            fg = final_g if l == depth - 1 else None
            h, routing, xp = merged
            h = _moe(h, routing, xp, router_w.shape[2], exp_w_gate, exp_w_up, exp_w_down, j, fg)
    if depth % 2 == 1:
        h = _final_norm(h, final_g)
    return h.reshape(batch, seq, d)
```

```python
import functools

import jax
import jax.numpy as jnp
from jax import lax
from jax.experimental import pallas as pl
from jax.experimental.pallas import tpu as pltpu

F32 = jnp.float32
BF16 = jnp.bfloat16
I32 = jnp.int32

EPS = 1e-6
CONV_K = 4
POOL_WINDOWS = (2, 4, 8, 16)
POOL_HALO = 16
TOP_K = 2
LANES = 128
SUBLANES = 8
NEG_BIG = -1e30
HI16 = -65536
VMEM_LIMIT = 56 * 1024 * 1024
VMEM_LIMIT_MERGE = 59 * 1024 * 1024

INPROJ_TM = 1024
INPROJ_TN = 1024
INPROJ_SUB = 256
MLSTM_T = 1024
MLSTM_CHUNK = 256
MERGE_TM = 512
MERGE_SUB = 512
FFN_TM = 1024
FFN_TF = 512
FFN_SUB = 256
MOE_TM = 1024
MOE_SUB = 256
COMBINE_TM = 256
NORM_TM = 1024
DMA_UNROLL = 8


def _cparams(n_axes, vmem_limit=VMEM_LIMIT):
    return pltpu.CompilerParams(
        dimension_semantics=("arbitrary",) * n_axes,
        vmem_limit_bytes=vmem_limit,
    )


def _rms(x, g):
    ms = jnp.mean(x * x, axis=-1, keepdims=True)
    return x * lax.rsqrt(ms + EPS) * g


def _sigmoid(x):
    return 1.0 / (1.0 + jnp.exp(-x))


def _inproj_kernel(x_ref, g_ref, w_ref, wif_ref, cw_ref, z_ref, gates_ref, xn_ref, halo_ref,
                   *, tiles_per_seq, n_q, n_qk, n_qkvo, k_scale):
    i = pl.program_id(0)
    j = pl.program_id(1)
    tm = x_ref.shape[0]
    n_v_end = n_qk + (n_qk - n_q)

    @pl.when(jnp.logical_and(i == 0, j == 0))
    def _():
        halo_ref[...] = jnp.zeros_like(halo_ref)

    @pl.when(j == 0)
    def _():
        xn = _rms(x_ref[...], g_ref[...]).astype(BF16)
        xn_ref[...] = xn
        gates_ref[...] = lax.dot_general(xn, wif_ref[...], (((1,), (1,)), ((), ())),
                                         preferred_element_type=F32)

    tn = z_ref.shape[1]
    sub = min(INPROJ_SUB, tn)
    col_blocks = [slice(c, c + sub) for c in range(0, tn, sub)]

    def project(cs):
        return lax.dot_general(xn_ref[...], w_ref[0, cs, :].astype(BF16),
                               (((1,), (1,)), ((), ())), preferred_element_type=F32)

    @pl.when(j < n_qk)
    def _():
        first = i % tiles_per_seq == 0
        scale = jnp.where(j >= n_q, k_scale, 1.0)
        for cs in col_blocks:
            acc = project(cs)
            halo = jnp.where(first, 0.0, halo_ref[j, :, cs])
            ext = jnp.concatenate([halo, acc], axis=0)
            w = cw_ref[:, cs]
            y = w[CONV_K - 1:CONV_K] * acc
            for d in range(1, CONV_K):
                y = y + w[CONV_K - 1 - d:CONV_K - d] * pltpu.roll(ext, d, axis=0)[SUBLANES:]
            halo_ref[j, :, cs] = acc[tm - SUBLANES:]
            hy = y * (0.5 * scale)
            y = hy + hy * jnp.tanh(0.5 * y)
            z_ref[:, cs] = y.astype(BF16)

    @pl.when(jnp.logical_and(j >= n_v_end, j < n_qkvo))
    def _():
        for cs in col_blocks:
            z_ref[:, cs] = _sigmoid(project(cs)).astype(BF16)

    @pl.when(jnp.logical_or(jnp.logical_and(j >= n_qk, j < n_v_end), j >= n_qkvo))
    def _():
        z_ref[...] = project(slice(None)).astype(BF16)


def _inproj(h, g, w_t, w_if, conv_w, layer, *, seq, dm, dh, col_p, col_g, dp):
    n, d = h.shape
    tm = min(INPROJ_TM, n, seq)
    tn = min(INPROJ_TN, dm)
    cols = 4 * dm + 2 * d + dp
    assert seq % tm == 0 and dm % tn == 0 and (2 * d) % tn == 0 and dp % tn == 0
    assert col_p % SUBLANES == 0 and col_g % SUBLANES == 0
    n_q = dm // tn
    n_qkvo = 4 * n_q
    n_gate = (2 * d) // tn
    cw = jnp.zeros((SUBLANES, 2 * dm), F32).at[:CONV_K].set(conv_w)

    def src_row(j):
        row = jnp.where(j < n_qkvo, j * tn,
                        jnp.where(j < n_qkvo + n_gate, col_g + (j - n_qkvo) * tn,
                                  col_p + (j - n_qkvo - n_gate) * tn))
        return pl.multiple_of(row, SUBLANES)

    return pl.pallas_call(
        functools.partial(_inproj_kernel, tiles_per_seq=seq // tm, n_q=n_q, n_qk=2 * n_q,
                          n_qkvo=n_qkvo, k_scale=dh ** -0.5),
        grid=(n // tm, cols // tn),
        in_specs=[
            pl.BlockSpec((tm, d), lambda i, j: (i, 0)),
            pl.BlockSpec((1, d), lambda i, j: (0, 0)),
            pl.BlockSpec((pl.Element(1), pl.Element(tn), pl.Element(d)),
                         lambda i, j: (layer, src_row(j), 0)),
            pl.BlockSpec((None, LANES, d), lambda i, j: (layer, 0, 0)),
            pl.BlockSpec((SUBLANES, tn), lambda i, j: (0, jnp.minimum(j, 2 * n_q - 1))),
        ],
        out_specs=[
            pl.BlockSpec((tm, tn), lambda i, j: (i, j)),
            pl.BlockSpec((tm, LANES), lambda i, j: (i, 0)),
        ],
        out_shape=[
            jax.ShapeDtypeStruct((n, cols), BF16),
            jax.ShapeDtypeStruct((n, LANES), F32),
        ],
        scratch_shapes=[pltpu.VMEM((tm, d), BF16),
                        pltpu.VMEM((2 * n_q, SUBLANES, tn), F32)],
        compiler_params=_cparams(2),
        name="inproj",
    )(h, g.reshape(1, d), w_t, w_if, cw)


def _mlstm_kernel(q_ref, k_ref, v_ref, og_ref, gt_ref, gb_ref, ng_ref,
                  out_ref, ct_ref, m_ref, *, chunk, nheads):
    t_rows, dm = q_ref.shape
    dh = dm // nheads

    @pl.when(pl.program_id(1) == 0)
    def _():
        ct_ref[...] = jnp.zeros_like(ct_ref)
        m_ref[...] = jnp.zeros_like(m_ref)

    gts = gt_ref[...] + gb_ref[...]
    lf = jnp.minimum(gts, 0.0) - jnp.log(1.0 + jnp.exp(-jnp.abs(gts)))
    in_chunk = lax.broadcasted_iota(I32, gts.shape, 0) & (chunk - 1)
    b = lf
    d = 1
    while d < chunk:
        b = b + jnp.where(in_chunk >= d, pltpu.roll(b, d, axis=0), 0.0)
        d *= 2
    u = gts - pltpu.roll(b, LANES - nheads, axis=1)
    d = 1
    while d < chunk:
        u = jnp.maximum(u, jnp.where(in_chunk >= d, pltpu.roll(u, d, axis=0), NEG_BIG))
        d *= 2
    gts_t = gts.T
    b_t = b.T

    tt = lax.broadcasted_iota(I32, (chunk, chunk), 0)
    ss = lax.broadcasted_iota(I32, (chunk, chunk), 1)
    causal = tt >= ss

    def wide(x, reps):
        return jnp.concatenate([x] * reps, axis=1)

    reps = dh // LANES
    ones_blk = jnp.ones((chunk, LANES), BF16)
    mean_blk = jnp.full((dh, LANES), 1.0 / dh, BF16)
    state = [ct_ref[hd] for hd in range(nheads)]
    m_run = [m_ref[hd] for hd in range(nheads)]
    i_rep = [jnp.broadcast_to(gts[:, hd:hd + 1], gts.shape) for hd in range(nheads)]
    b_rep = [jnp.broadcast_to(b[:, nheads + hd:nheads + hd + 1], gts.shape)
             for hd in range(nheads)]
    u_rep = [jnp.broadcast_to(u[:, hd:hd + 1], gts.shape) for hd in range(nheads)]
    for c in range(t_rows // chunk):
        lo, hi = c * chunk, (c + 1) * chunk
        for hd in range(nheads):
            cs = slice(hd * dh, (hd + 1) * dh)
            qb, kb = q_ref[lo:hi, cs], k_ref[lo:hi, cs]
            v_ext = jnp.concatenate([v_ref[lo:hi, cs], ones_blk], axis=1)
            bc, ic, uc = b_rep[hd][lo:hi], i_rep[hd][lo:hi], u_rep[hd][lo:hi]
            br, ir = b_t[nheads + hd:nheads + hd + 1, lo:hi], gts_t[hd:hd + 1, lo:hi]
            m_prev = m_run[hd]
            inter = bc + m_prev
            mt = jnp.maximum(inter, bc + uc)
            wi = jnp.exp(inter - mt)
            dmat = jnp.where(causal, wide(bc, chunk // LANES) - br + ir, NEG_BIG)
            s = lax.dot_general(qb, kb, (((1,), (1,)), ((), ())),
                                preferred_element_type=F32)
            s = s * jnp.exp(dmat - wide(mt, chunk // LANES))
            nd = wide(wi, reps + 1) * jnp.dot(qb, state[hd].astype(BF16),
                                              preferred_element_type=F32)
            nd = nd + jnp.dot(s.astype(BF16), v_ext, preferred_element_type=F32)
            num, den = nd[:, :dh], nd[:, dh:]
            rden = 1.0 / jnp.maximum(jnp.abs(den), jnp.exp(-mt))
            ms = jnp.dot((num * num).astype(BF16), mean_blk, preferred_element_type=F32)
            scale = rden * lax.rsqrt(ms * (rden * rden) + EPS)
            hn = num * wide(scale, reps) * ng_ref[:, cs]
            out_ref[lo:hi, cs] = (og_ref[lo:hi, cs].astype(F32) * hn).astype(BF16)
            bl = bc[chunk - 1:chunk]
            gcol = bl - bc + ic
            m_new = jnp.maximum(bl + m_prev, jnp.max(gcol, axis=0, keepdims=True))
            a = jnp.exp(bl + m_prev - m_new)
            kw = kb.astype(F32) * wide(jnp.exp(gcol - m_new), reps)
            state[hd] = wide(a, reps + 1) * state[hd] + lax.dot_general(
                kw.astype(BF16), v_ext, (((0,), (0,)), ((), ())), preferred_element_type=F32)
            m_run[hd] = m_new
    for hd in range(nheads):
        ct_ref[hd] = state[hd]
        m_ref[hd] = m_run[hd]


def _mlstm(z, gates, gate_b, norm_g, *, batch, seq, nheads, dh):
    n = z.shape[0]
    t = min(MLSTM_T, seq)
    chunk = min(MLSTM_CHUNK, t)
    spb = seq // t
    dm = nheads * dh
    gb = jnp.zeros((1, LANES), F32).at[0, :2 * nheads].set(gate_b)

    def zspec(col_block):
        return pl.BlockSpec((t, dm), lambda b, s: (b * spb + s, col_block))

    return pl.pallas_call(
        functools.partial(_mlstm_kernel, chunk=chunk, nheads=nheads),
        grid=(batch, spb),
        in_specs=[
            zspec(0), zspec(1), zspec(2), zspec(3),
            pl.BlockSpec((t, LANES), lambda b, s: (b * spb + s, 0)),
            pl.BlockSpec((1, LANES), lambda b, s: (0, 0)),
            pl.BlockSpec((1, dm), lambda b, s: (0, 0)),
        ],
        out_specs=pl.BlockSpec((t, dm), lambda b, s: (b * spb + s, 0)),
        out_shape=jax.ShapeDtypeStruct((n, dm), BF16),
        scratch_shapes=[
            pltpu.VMEM((nheads, dh, dh + LANES), F32),
            pltpu.VMEM((nheads, 1, LANES), F32),
        ],
        compiler_params=_cparams(2),
        name="mlstm",
    )(z, z, z, z, gates, gb, norm_g.reshape(1, dm))


def _pool_mix(p_ref, halo_ref, w_ref, sc_ref, pos0):
    t_rows = p_ref.shape[0]
    grp = w_ref.shape[1]
    p = p_ref[...].astype(F32)
    halo = jnp.where(pos0 == 0, 0.0, halo_ref[...].astype(F32))
    ext = jnp.concatenate([halo, p], axis=0)
    pos = pos0 + lax.broadcasted_iota(I32, (t_rows, 1), 0)
    posf = (pos + 1).astype(F32)
    outs = []
    for j, w in enumerate(POOL_WINDOWS):
        cs = slice(j * grp, (j + 1) * grp)
        win = ext[:, cs]
        d = 1
        while d < w:
            win = win + pltpu.roll(win, d, axis=0)
            d *= 2
        pooled = win[POOL_HALO:] / jnp.minimum(posf, float(w)) - p[:, cs]
        y = jnp.dot(pooled.astype(BF16), w_ref[j], preferred_element_type=F32)
        outs.append((y * sc_ref[:, cs]).astype(BF16))
    return jnp.concatenate(outs, axis=1)


def _route(xn, wh, wl, rb, nexp):
    xh = xn.astype(BF16)
    xhf = xh.astype(F32)
    xl = (xn - xhf).astype(BF16)
    half = xn.shape[1] // 2
    lo = lax.shift_right_logical(lax.bitcast_convert_type(xhf[:, :half], I32), 16)
    hi = lax.bitcast_convert_type(xhf[:, half:], I32) & HI16
    logits = (jnp.dot(xh, wh, preferred_element_type=F32)
              + jnp.dot(xh, wl, preferred_element_type=F32)
              + jnp.dot(xl, wh, preferred_element_type=F32)) + rb
    lane = lax.broadcasted_iota(I32, logits.shape, 1)
    logits = jnp.where(lane < nexp, logits, NEG_BIG)
    v1 = jnp.max(logits, axis=1, keepdims=True)
    i1 = jnp.min(jnp.where(logits == v1, lane, LANES), axis=1, keepdims=True)
    rest = jnp.where(lane == i1, NEG_BIG, logits)
    v2 = jnp.max(rest, axis=1, keepdims=True)
    i2 = jnp.min(jnp.where(rest == v2, lane, LANES), axis=1, keepdims=True)
    e2 = jnp.exp(v2 - v1)
    g1 = 1.0 / (1.0 + e2)
    g2 = e2 / (1.0 + e2)
    out = jnp.where(lane == 0, i1.astype(F32), 0.0)
    out = jnp.where(lane == 1, i2.astype(F32), out)
    out = jnp.where(lane == 2, g1, out)
    out = jnp.where(lane == 3, g2, out)
    return out, hi | lo


def _merge_kernel(*refs, tiles_per_seq, nexp):
    (h_ref, hm_ref, p_ref, halo_ref, gm_ref, gp_ref, pw_ref, psc_ref,
     wbm_ref, wbp_ref, wo_ref) = refs[:11]
    tm = h_ref.shape[0]
    pos0 = (pl.program_id(0) % tiles_per_seq) * tm
    hp = _pool_mix(p_ref, halo_ref, pw_ref, psc_ref, pos0)
    hm = hm_ref[...]
    d = h_ref.shape[1]
    sub = min(MERGE_SUB, d)
    out = h_ref[...]
    for c in range(0, d, sub):
        cs = slice(c, c + sub)
        ym = jnp.dot(hm, wbm_ref[:, cs], preferred_element_type=F32)
        yp = jnp.dot(hp, wbp_ref[:, cs], preferred_element_type=F32)
        y = (_sigmoid(gm_ref[:, cs].astype(F32)) * ym
             + _sigmoid(gp_ref[:, cs].astype(F32)) * yp)
        out = out + jnp.dot(y.astype(BF16), wo_ref[cs, :], preferred_element_type=F32)
    if nexp is None:
        refs[11][...] = out
    else:
        n2g_ref, rwh_ref, rwl_ref, rb_ref, out_ref, rt_ref, xp_ref = refs[11:]
        out_ref[...] = out
        rt_ref[...], xp_ref[...] = _route(_rms(out, n2g_ref[...]), rwh_ref[...],
                                          rwl_ref[...], rb_ref[...], nexp)


def _merge(h, hm, z, pool_w, pool_scale, wbm, wbp, wo, layer, router, *,
           seq, gate_block, pool_block):
    n, d = h.shape
    dm = hm.shape[1]
    ngrp, grp, _ = pool_w.shape
    dp = ngrp * grp
    tm = min(MERGE_TM, n, seq)
    assert seq % tm == 0 and tm % POOL_HALO == 0
    hb = tm // POOL_HALO
    once = pl.Buffered(1)
    in_specs = [
        pl.BlockSpec((tm, d), lambda i: (i, 0)),
        pl.BlockSpec((tm, dm), lambda i: (i, 0)),
        pl.BlockSpec((tm, dp), lambda i: (i, pool_block)),
        pl.BlockSpec((POOL_HALO, dp), lambda i: (jnp.maximum(i * hb - 1, 0), pool_block)),
        pl.BlockSpec((tm, d), lambda i: (i, gate_block)),
        pl.BlockSpec((tm, d), lambda i: (i, gate_block + 1)),
        pl.BlockSpec((ngrp, grp, grp), lambda i: (0, 0, 0)),
        pl.BlockSpec((1, dp), lambda i: (0, 0)),
        pl.BlockSpec((None, dm, d), lambda i: (layer, 0, 0), pipeline_mode=once),
        pl.BlockSpec((None, dp, d), lambda i: (layer, 0, 0), pipeline_mode=once),
        pl.BlockSpec((None, d, d), lambda i: (layer, 0, 0), pipeline_mode=once),
    ]
    args = [h, hm, z, z, z, z, pool_w, pool_scale.reshape(1, dp), wbm, wbp, wo]
    out_specs = [pl.BlockSpec((tm, d), lambda i: (i, 0))]
    out_shape = [jax.ShapeDtypeStruct((n, d), F32)]
    nexp = None
    if router is not None:
        g2, router_w, router_b = router
        nexp = router_w.shape[1]
        w = jnp.zeros((d, LANES), F32).at[:, :nexp].set(router_w)
        wh = w.astype(BF16)
        wl = (w - wh.astype(F32)).astype(BF16)
        rb = jnp.zeros((1, LANES), F32).at[0, :nexp].set(router_b)
        in_specs += [
            pl.BlockSpec((1, d), lambda i: (0, 0)),
            pl.BlockSpec((d, LANES), lambda i: (0, 0)),
            pl.BlockSpec((d, LANES), lambda i: (0, 0)),
            pl.BlockSpec((1, LANES), lambda i: (0, 0)),
        ]
        args += [g2.reshape(1, d), wh, wl, rb]
        out_specs += [pl.BlockSpec((tm, LANES), lambda i: (i, 0)),
                      pl.BlockSpec((tm, d // 2), lambda i: (i, 0))]
        out_shape += [jax.ShapeDtypeStruct((n, LANES), F32),
                      jax.ShapeDtypeStruct((n, d // 2), I32)]
    return pl.pallas_call(
        functools.partial(_merge_kernel, tiles_per_seq=seq // tm, nexp=nexp),
        grid=(n // tm,),
        in_specs=in_specs,
        out_specs=out_specs,
        out_shape=out_shape,
        compiler_params=_cparams(1, VMEM_LIMIT_MERGE),
        name="merge",
    )(*args)


def _swiglu_step(xn_ref, wg_ref, wu_ref, wd_ref, out_ref, rows=slice(None)):
    xn = xn_ref[rows, :]
    tf = wg_ref.shape[1]
    sub = min(FFN_SUB, tf)
    for c in range(0, tf, sub):
        cs = slice(c, c + sub)
        a = jnp.dot(xn, wg_ref[:, cs].astype(BF16), preferred_element_type=F32)
        u = jnp.dot(xn, wu_ref[:, cs].astype(BF16), preferred_element_type=F32)
        act = (a * _sigmoid(a) * u).astype(BF16)
        out_ref[rows, :] += jnp.dot(act, wd_ref[cs, :].astype(BF16),
                                    preferred_element_type=F32)


def _ffn_kernel(x_hbm, g_ref, wg_ref, wu_ref, wd_ref, out_ref, xn_ref, sem):
    tm = out_ref.shape[0]

    @pl.when(pl.program_id(1) == 0)
    def _():
        rows = pl.ds(pl.multiple_of(pl.program_id(0) * tm, tm), tm)
        copy = pltpu.make_async_copy(x_hbm.at[rows], out_ref, sem)
        copy.start()
        copy.wait()
        xn_ref[...] = _rms(out_ref[...], g_ref[...]).astype(BF16)

    _swiglu_step(xn_ref, wg_ref, wu_ref, wd_ref, out_ref)


def _ffn(h, g, wg, wu, wd, j):
    n, d = h.shape
    ff = wg.shape[2]
    tm = min(FFN_TM, n)
    tf = min(FFN_TF, ff)
    return pl.pallas_call(
        _ffn_kernel,
        grid=(n // tm, ff // tf),
        in_specs=[
            pl.BlockSpec(memory_space=pl.ANY),
            pl.BlockSpec((1, d), lambda i, f: (0, 0)),
            pl.BlockSpec((None, d, tf), lambda i, f: (j, 0, f)),
            pl.BlockSpec((None, d, tf), lambda i, f: (j, 0, f)),
            pl.BlockSpec((None, tf, d), lambda i, f: (j, f, 0)),
        ],
        out_specs=pl.BlockSpec((tm, d), lambda i, f: (i, 0)),
        out_shape=jax.ShapeDtypeStruct((n, d), F32),
        scratch_shapes=[pltpu.VMEM((tm, d), BF16), pltpu.SemaphoreType.DMA(())],
        compiler_params=_cparams(2),
        name="ffn",
    )(h, g.reshape(1, d), wg, wu, wd)


def _gffn_kernel(te_ref, nv_ref, cnt_ref, src_ref, xp_hbm, wg_ref, wu_ref, wd_ref, out_ref,
                 xg_ref, xn_ref, sem, *, tm, sub, issue_steps):
    t = pl.program_id(0)
    f = pl.program_id(1)
    nf = pl.num_programs(1)
    nv = nv_ref[0]
    cnt = cnt_ref[t]
    half = xg_ref.shape[2]
    rows_per_step = tm // issue_steps
    weights = (wg_ref, wu_ref, wd_ref)

    def slot_copy(slot):
        return pltpu.make_async_copy(xp_hbm.at[pl.ds(0, tm)], xg_ref.at[slot], sem.at[slot])

    def row_copy(tile, slot, r):
        tok = src_ref[tile * tm + r]
        return pltpu.make_async_copy(xp_hbm.at[pl.ds(tok, 1)],
                                     xg_ref.at[slot, pl.ds(r, 1)], sem.at[slot])

    def prefetch_rows():
        nxt = jnp.minimum(t + 1, nv - 1)
        slot = (t + 1) % 2
        r0 = f * rows_per_step
        for r in range(rows_per_step):
            row_copy(nxt, slot, r0 + r).start()

    @pl.when(t < nv)
    def _():
        @pl.when(f == 0)
        def _():
            @pl.when(t == 0)
            def _():
                def body(r, carry):
                    row_copy(0, 0, r).start()
                    return carry

                lax.fori_loop(0, tm, body, 0, unroll=DMA_UNROLL)

            slot_copy(t % 2).wait()
            p = xg_ref[t % 2]
            xn_ref[:, :half] = lax.bitcast_convert_type(lax.shift_left(p, 16), F32).astype(BF16)
            xn_ref[:, half:] = lax.bitcast_convert_type(p & HI16, F32).astype(BF16)
            out_ref[...] = jnp.zeros_like(out_ref)

        full = cnt == tm
        issuing = f < issue_steps

        @pl.when(jnp.logical_and(full, issuing))
        def _():
            prefetch_rows()
            _swiglu_step(xn_ref, *weights, out_ref)

        @pl.when(jnp.logical_and(full, jnp.logical_not(issuing)))
        def _():
            _swiglu_step(xn_ref, *weights, out_ref)

        @pl.when(jnp.logical_not(full))
        def _():
            @pl.when(issuing)
            def _():
                prefetch_rows()

            for q in range(tm // sub):
                @pl.when(q * sub < cnt)
                def _():
                    _swiglu_step(xn_ref, *weights, out_ref, rows=slice(q * sub, (q + 1) * sub))

        @pl.when(jnp.logical_and(t == nv - 1, f == nf - 1))
        def _():
            slot_copy((t + 1) % 2).wait()

    @pl.when(jnp.logical_and(t >= nv, f == 0))
    def _():
        out_ref[...] = jnp.zeros_like(out_ref)


def _grouped_ffn(xp, wg, wu, wd, j, tile_expert, n_valid, tile_rows, src_token, *, tm):
    half = xp.shape[1]
    d = 2 * half
    slots = src_token.shape[0]
    ff = wg.shape[3]
    tf = min(FFN_TF, ff)
    nf = ff // tf
    ntiles = slots // tm
    sub = min(MOE_SUB, tm)
    issue_steps = 1
    while issue_steps * 2 <= nf and tm % (issue_steps * 2) == 0:
        issue_steps *= 2

    def fcol(t, f, nv):
        return jnp.where(t < nv[0], f, nf - 1)

    grid_spec = pltpu.PrefetchScalarGridSpec(
        num_scalar_prefetch=4,
        grid=(ntiles, nf),
        in_specs=[
            pl.BlockSpec(memory_space=pl.ANY),
            pl.BlockSpec((None, None, d, tf),
                         lambda t, f, te, nv, cnt, src: (j, te[t], 0, fcol(t, f, nv))),
            pl.BlockSpec((None, None, d, tf),
                         lambda t, f, te, nv, cnt, src: (j, te[t], 0, fcol(t, f, nv))),
            pl.BlockSpec((None, None, tf, d),
                         lambda t, f, te, nv, cnt, src: (j, te[t], fcol(t, f, nv), 0)),
        ],
        out_specs=pl.BlockSpec((tm, d), lambda t, f, te, nv, cnt, src: (t, 0),
                               pipeline_mode=pl.Buffered(1)),
        scratch_shapes=[
            pltpu.VMEM((2, tm, half), I32),
            pltpu.VMEM((tm, d), BF16),
            pltpu.SemaphoreType.DMA((2,)),
        ],
    )
    return pl.pallas_call(
        functools.partial(_gffn_kernel, tm=tm, sub=sub, issue_steps=issue_steps),
        grid_spec=grid_spec,
        out_shape=jax.ShapeDtypeStruct((slots, d), F32),
        compiler_params=_cparams(2),
        name="grouped_ffn",
    )(tile_expert, n_valid, tile_rows, src_token, xp, wg, wu, wd)


def _combine_kernel(slot_ref, h_ref, rt_ref, g_ref, ys_hbm, out_ref, buf_ref, sem,
                    *, tm, ntok, final_norm):
    i = pl.program_id(0)
    nt = pl.num_programs(0)

    def row_copy(tile, slot, kk, r):
        sl = slot_ref[kk * ntok + tile * tm + r]
        return pltpu.make_async_copy(ys_hbm.at[pl.ds(sl, 1)],
                                     buf_ref.at[slot, kk, pl.ds(r, 1)], sem.at[slot])

    def issue(tile, slot):
        for r in range(tm):
            for kk in range(TOP_K):
                row_copy(tile, slot, kk, r).start()

    @pl.when(i == 0)
    def _():
        issue(0, 0)

    for parity in range(2):
        @pl.when(jnp.logical_and(i + 1 < nt, (i + 1) % 2 == parity))
        def _(parity=parity):
            issue(i + 1, parity)

    slot = i % 2
    for kk in range(TOP_K):
        pltpu.make_async_copy(ys_hbm.at[pl.ds(0, tm)], buf_ref.at[slot, kk], sem.at[slot]).wait()
    rt = rt_ref[...]
    out = h_ref[...] + rt[:, 2:3] * buf_ref[slot, 0] + rt[:, 3:4] * buf_ref[slot, 1]
    if final_norm:
        out = _rms(out, g_ref[...])
    out_ref[...] = out


def _combine(h, routing, ys, slot_of, final_g, *, tm):
    n, d = h.shape
    g = jnp.ones((1, d), F32) if final_g is None else final_g.reshape(1, d)
    grid_spec = pltpu.PrefetchScalarGridSpec(
        num_scalar_prefetch=1,
        grid=(n // tm,),
        in_specs=[
            pl.BlockSpec((tm, d), lambda i, sl: (i, 0)),
            pl.BlockSpec((tm, LANES), lambda i, sl: (i, 0)),
            pl.BlockSpec((1, d), lambda i, sl: (0, 0)),
            pl.BlockSpec(memory_space=pl.ANY),
        ],
        out_specs=pl.BlockSpec((tm, d), lambda i, sl: (i, 0)),
        scratch_shapes=[pltpu.VMEM((2, TOP_K, tm, d), F32), pltpu.SemaphoreType.DMA((2,))],
    )
    return pl.pallas_call(
        functools.partial(_combine_kernel, tm=tm, ntok=n, final_norm=final_g is not None),
        grid_spec=grid_spec,
        out_shape=jax.ShapeDtypeStruct((n, d), F32),
        compiler_params=_cparams(1),
        name="combine",
    )(slot_of, h, routing, g, ys)


def _moe(h, routing, xp, nexp, wg, wu, wd, j, final_g):
    n, d = h.shape
    tm = min(MOE_TM, n)
    experts = routing[:, :TOP_K].astype(I32).T.reshape(-1)
    onehot = (experts[:, None] == jnp.arange(nexp, dtype=I32)[None, :]).astype(I32)
    rank = jnp.sum((jnp.cumsum(onehot, axis=0) - onehot) * onehot, axis=1)
    counts = jnp.sum(onehot, axis=0)
    tiles_per = (counts + tm - 1) // tm
    tile_end = jnp.cumsum(tiles_per)
    tile_start = tile_end - tiles_per
    slot_of = (tile_start[experts] * tm + rank).astype(I32)
    max_tiles = (TOP_K * n) // tm + nexp
    tokens = jnp.tile(jnp.arange(n, dtype=I32), TOP_K)
    n_valid = tile_end[-1:].astype(I32)
    tile_ids = jnp.arange(max_tiles, dtype=I32)
    tile_expert = jnp.minimum(
        jnp.sum((tile_ids[:, None] >= tile_end[None, :]).astype(I32), axis=1), nexp - 1)
    tile_rows = jnp.clip(counts[tile_expert] - (tile_ids - tile_start[tile_expert]) * tm,
                         0, tm).astype(I32)
    slots = max_tiles * tm
    row = jnp.arange(tm, dtype=I32)[None, :]
    last_rows = (counts - (tiles_per - 1) * tm)[:, None]
    last_slot = ((tile_end - 1) * tm)[:, None] + row
    past_end = slots + jnp.arange(nexp * tm, dtype=I32).reshape(nexp, tm)
    pad_keys = jnp.where(row >= last_rows, last_slot, past_end).reshape(-1)
    keys = jnp.concatenate([slot_of, pad_keys.astype(I32)])
    vals = jnp.concatenate([tokens, jnp.zeros((nexp * tm,), I32)])
    src_token = lax.sort_key_val(keys, vals)[1]
    ys = _grouped_ffn(xp, wg, wu, wd, j, tile_expert, n_valid, tile_rows, src_token, tm=tm)
    return _combine(h, routing, ys, slot_of, final_g, tm=min(COMBINE_TM, n))


def _norm_kernel(x_ref, g_ref, out_ref):
    out_ref[...] = _rms(x_ref[...], g_ref[...])


def _final_norm(h, g):
    n, d = h.shape
    tm = min(NORM_TM, n)
    return pl.pallas_call(
        _norm_kernel,
        grid=(n // tm,),
        in_specs=[pl.BlockSpec((tm, d), lambda i: (i, 0)),
                  pl.BlockSpec((1, d), lambda i: (0, 0))],
        out_specs=pl.BlockSpec((tm, d), lambda i: (i, 0)),
        out_shape=jax.ShapeDtypeStruct((n, d), F32),
        compiler_params=_cparams(1),
        name="final_norm",
    )(h, g.reshape(1, d))


def kernel(x, norm1_g, norm2_g, final_g, w_in, mlstm_gate_b, conv_w, mlstm_norm_g, pool_w, pool_scale, w_branch_m, w_branch_p, w_out, ffn_w_gate, ffn_w_up, ffn_w_down, router_w, router_b, exp_w_gate, exp_w_up, exp_w_down):
    batch, seq, d = x.shape
    depth = w_in.shape[0]
    nheads = mlstm_gate_b.shape[1] // 2
    dm = conv_w.shape[2] // 2
    dh = dm // nheads
    dp = pool_scale.shape[1]
    col_o_end = 4 * dm
    col_p = col_o_end + 2 * nheads
    col_g = col_p + dp
    assert col_o_end % d == 0 and (col_o_end + 2 * d) % dp == 0
    w_t = jnp.swapaxes(w_in, 1, 2)
    w_if = jnp.zeros((depth, LANES, d), F32).at[:, :2 * nheads].set(
        w_t[:, col_o_end:col_p]).astype(BF16)
    gate_block = col_o_end // d
    pool_block = (col_o_end + 2 * d) // dp
    wbm = w_branch_m.astype(BF16)
    wbp = w_branch_p.astype(BF16)
    wo = w_out.astype(BF16)
    pw = pool_w.astype(BF16)

    h = x.reshape(batch * seq, d)
    for l in range(depth):
        z, gates = _inproj(h, norm1_g[l], w_t, w_if, conv_w[l], l, seq=seq, dm=dm, dh=dh,
                           col_p=col_p, col_g=col_g, dp=dp)
        hm = _mlstm(z, gates, mlstm_gate_b[l], mlstm_norm_g[l],
                    batch=batch, seq=seq, nheads=nheads, dh=dh)
        j = l // 2
        router = None if l % 2 == 0 else (norm2_g[l], router_w[j], router_b[j])
        merged = _merge(h, hm, z, pw[l], pool_scale[l], wbm, wbp, wo, l, router,
                        seq=seq, gate_block=gate_block, pool_block=pool_block)
        if l % 2 == 0:
            h = _ffn(merged[0], norm2_g[l], ffn_w_gate, ffn_w_up, ffn_w_down, j)
        else:
            fg = final_g if l == depth - 1 else None
            h, routing, xp = merged
            h = _moe(h, routing, xp, router_w.shape[2], exp_w_gate, exp_w_up, exp_w_down, j, fg)
    if depth % 2 == 1:
        h = _final_norm(h, final_g)
    return h.reshape(batch, seq, d)
```
